```python
import jax, jax.numpy as jnp
from jax import lax
import numpy as np

D_MODEL = 4096
BATCH = 2
SEQ = 4096
DEPTH = 2

N_A_LAYERS = DEPTH // 2
N_B_LAYERS = DEPTH - N_A_LAYERS
HEAD_DIM = 128
A_HEADS = D_MODEL // HEAD_DIM
A_WIDTH = A_HEADS * HEAD_DIM
B_Q_HEADS = D_MODEL // HEAD_DIM
B_KV_HEADS = 4
B_GROUP = B_Q_HEADS // B_KV_HEADS
B_WIDTH = B_Q_HEADS * HEAD_DIM
KV_WIDTH = B_KV_HEADS * HEAD_DIM
WINDOW = 128
BLOCK = 128
ROPE_THETA = 10000.0
EPS = 1e-6

kernel_name = "yoco_fox_swa_sink_adaln_hybrid"


def rms_norm(x, g):
    xf = x.astype(jnp.float32)
    y = xf * lax.rsqrt(jnp.mean(xf * xf, axis=-1, keepdims=True) + EPS)
    return (y * g.astype(jnp.float32)).astype(x.dtype)


def rope(x, pos):
    half = HEAD_DIM // 2
    inv = ROPE_THETA ** (-jnp.arange(half, dtype=jnp.float32) / half)
    ang = pos.astype(jnp.float32)[:, None] * inv[None, :]
    cos = jnp.cos(ang)[None, :, None, :]
    sin = jnp.sin(ang)[None, :, None, :]
    xf = x.astype(jnp.float32)
    x1, x2 = xf[..., :half], xf[..., half:]
    return jnp.concatenate([x1 * cos - x2 * sin, x2 * cos + x1 * sin], axis=-1).astype(x.dtype)


def adaln(c, w_ada, b_ada):
    mod = jax.nn.silu(c) @ w_ada + b_ada
    shift, scale, gate = jnp.split(mod, 3, axis=-1)
    return shift[:, None, :], scale[:, None, :], gate[:, None, :]


def forgetting_attention(q, k, v, log_f):
    B, H, S, d = q.shape
    n_blocks = S // BLOCK
    F = jnp.cumsum(log_f, axis=-1)
    key_pos = jnp.arange(S)
    scale = HEAD_DIM ** -0.5

    def one_block(i):
        start = i * BLOCK
        qb = lax.dynamic_slice_in_dim(q, start, BLOCK, axis=2)
        Fq = lax.dynamic_slice_in_dim(F, start, BLOCK, axis=2)
        s = jnp.einsum('bhqd,bhkd->bhqk', qb, k).astype(jnp.float32) * scale
        s = s + (Fq[..., :, None] - F[..., None, :])
        q_pos = start + jnp.arange(BLOCK)
        causal = key_pos[None, :] <= q_pos[:, None]
        s = jnp.where(causal[None, None], s, -jnp.inf)
        p = jax.nn.softmax(s, axis=-1).astype(v.dtype)
        return jnp.einsum('bhqk,bhkd->bhqd', p, v)

    out = lax.map(one_block, jnp.arange(n_blocks))
    return out.transpose(1, 2, 0, 3, 4).reshape(B, H, S, d)


def sliding_window_sink_attention(q, k, v, sinks):
    B, S = q.shape[0], q.shape[1]
    nb = S // BLOCK
    qb = q.reshape(B, nb, BLOCK, B_KV_HEADS, B_GROUP, HEAD_DIM)

    def band(t):
        tb = t.reshape(B, nb, BLOCK, B_KV_HEADS, HEAD_DIM)
        prev = jnp.pad(tb[:, :-1], ((0, 0), (1, 0), (0, 0), (0, 0), (0, 0)))
        return jnp.concatenate([prev, tb], axis=2)

    kb, vb = band(k), band(v)
    s = jnp.einsum('bnqkgd,bnskd->bnkgqs', qb, kb).astype(jnp.float32) * (HEAD_DIM ** -0.5)
    qi = jnp.arange(BLOCK)[:, None]
    sj = jnp.arange(2 * BLOCK)[None, :]
    diff = qi + BLOCK - sj
    in_window = (diff >= 0) & (diff < WINDOW)
    key_abs = jnp.arange(nb)[:, None] * BLOCK + jnp.arange(2 * BLOCK)[None, :] - BLOCK
    valid = in_window[None] & (key_abs >= 0)[:, None, :]
    s = jnp.where(valid[None, :, None, None], s, -jnp.inf)
    sink = jnp.broadcast_to(
        sinks.astype(jnp.float32).reshape(B_KV_HEADS, B_GROUP)[None, None, :, :, None, None],
        s.shape[:-1] + (1,))
    p = jax.nn.softmax(jnp.concatenate([s, sink], axis=-1), axis=-1)[..., :-1].astype(v.dtype)
    o = jnp.einsum('bnkgqs,bnskd->bnqkgd', p, vb)
    return o.reshape(B, S, B_Q_HEADS, HEAD_DIM)


def fox_layer(x, c, g, w_ada, b_ada, w_in, b_f, w_out):
    B, S, _ = x.shape
    shift, scale, gate = adaln(c, w_ada, b_ada)
    h = rms_norm(x, g) * (1 + scale) + shift
    proj = h @ w_in
    q, k, v, z, f = jnp.split(proj, [A_WIDTH, 2 * A_WIDTH, 3 * A_WIDTH, 4 * A_WIDTH], axis=-1)
    heads = lambda t: t.reshape(B, S, A_HEADS, HEAD_DIM).transpose(0, 2, 1, 3)
    log_f = jax.nn.log_sigmoid(f.astype(jnp.float32) + b_f.astype(jnp.float32)).transpose(0, 2, 1)
    o = forgetting_attention(heads(q), heads(k), heads(v), log_f)
    o = o.transpose(0, 2, 1, 3).reshape(B, S, A_WIDTH)
    y = (o * jax.nn.silu(z)) @ w_out
    return x + gate * y


def shared_kv(h, g_kv, w_kv, pos):
    B, S, _ = h.shape
    kv = rms_norm(h, g_kv) @ w_kv
    k, v = jnp.split(kv, 2, axis=-1)
    k = rope(k.reshape(B, S, B_KV_HEADS, HEAD_DIM), pos)
    v = v.reshape(B, S, B_KV_HEADS, HEAD_DIM)
    return k, v


def swa_layer(x, c, g, w_ada, b_ada, w_in, sinks, w_out, k, v, pos):
    B, S, _ = x.shape
    shift, scale, gate = adaln(c, w_ada, b_ada)
    h = rms_norm(x, g) * (1 + scale) + shift
    proj = h @ w_in
    q, z = jnp.split(proj, 2, axis=-1)
    q = rope(q.reshape(B, S, B_Q_HEADS, HEAD_DIM), pos)
    o = sliding_window_sink_attention(q, k, v, sinks).reshape(B, S, B_WIDTH)
    y = (o * jax.nn.silu(z)) @ w_out
    return x + gate * y


def setup_inputs(seed: int = 0) -> dict:
    key = jax.random.key(seed)
    ks = jax.random.split(key, 14)
    f32 = jnp.float32
    nrm = lambda k, shape, s: jax.random.normal(k, shape, f32) * s
    d_in_a = 4 * A_WIDTH + A_HEADS
    return {
        "x": nrm(ks[0], (BATCH, SEQ, D_MODEL), 1.0),
        "c": nrm(ks[1], (BATCH, D_MODEL), 1.0),
        "norm_g": 1.0 + nrm(ks[2], (DEPTH, D_MODEL), 0.02),
        "ada_w": nrm(ks[3], (DEPTH, D_MODEL, 3 * D_MODEL), 0.5 * D_MODEL ** -0.5),
        "ada_b": nrm(ks[4], (DEPTH, 3 * D_MODEL), 0.02),
        "a_w_in": nrm(ks[5], (N_A_LAYERS, D_MODEL, d_in_a), D_MODEL ** -0.5),
        "a_b_f": jax.random.uniform(ks[6], (N_A_LAYERS, A_HEADS), f32, 1.0, 4.0),
        "a_w_out": nrm(ks[7], (N_A_LAYERS, A_WIDTH, D_MODEL), A_WIDTH ** -0.5),
        "kv_norm_g": 1.0 + nrm(ks[8], (D_MODEL,), 0.02),
        "kv_w": nrm(ks[9], (D_MODEL, 2 * KV_WIDTH), D_MODEL ** -0.5),
        "b_w_in": nrm(ks[10], (N_B_LAYERS, D_MODEL, 2 * B_WIDTH), D_MODEL ** -0.5),
        "b_sinks": nrm(ks[11], (N_B_LAYERS, B_Q_HEADS), 0.5),
        "b_w_out": nrm(ks[12], (N_B_LAYERS, B_WIDTH, D_MODEL), B_WIDTH ** -0.5),
        "final_norm_g": 1.0 + nrm(ks[13], (D_MODEL,), 0.02),
    }


def reference(x, c, norm_g, ada_w, ada_b, a_w_in, a_b_f, a_w_out, kv_norm_g, kv_w,
              b_w_in, b_sinks, b_w_out, final_norm_g):
    pos = jnp.arange(x.shape[1])
    h = x
    k = v = None
    for layer in range(DEPTH):
        if layer < N_A_LAYERS:
            h = fox_layer(h, c, norm_g[layer], ada_w[layer], ada_b[layer],
                          a_w_in[layer], a_b_f[layer], a_w_out[layer])
        else:
            if layer == N_A_LAYERS:
                k, v = shared_kv(h, kv_norm_g, kv_w, pos)
            j = layer - N_A_LAYERS
            h = swa_layer(h, c, norm_g[layer], ada_w[layer], ada_b[layer],
                          b_w_in[j], b_sinks[j], b_w_out[j], k, v, pos)
    return rms_norm(h, final_norm_g)
```

```python
import functools
import math

import jax
import jax.numpy as jnp
from jax import lax
from jax.experimental import pallas as pl
from jax.experimental.pallas import tpu as pltpu

HEAD_DIM = 128
KV_HEADS = 4
WINDOW = 128
ROPE_THETA = 10000.0
EPS = 1e-6

V7X_VMEM_BYTES = 64 * 1024 * 1024
VMEM_LIMIT_BIG = 56 * 1024 * 1024
VMEM_LIMIT_MID = 40 * 1024 * 1024
LANES = 128
SUBLANES = 8

F32 = jnp.float32
BF16 = jnp.bfloat16


def _params(limit, sem):
    return pltpu.CompilerParams(dimension_semantics=sem, vmem_limit_bytes=limit)


def _tile(n, pref):
    t = min(n, pref)
    assert n % t == 0, (n, t)
    return t


def _adaln_kernel(c_ref, w_ref, b_ref, o_ref):
    c = c_ref[...]
    sc = (c * jax.nn.sigmoid(c)).astype(BF16)
    acc = jnp.dot(sc, w_ref[...].astype(BF16), preferred_element_type=F32)
    o_ref[...] = acc + b_ref[...]


def _adaln(c, ada_w, ada_b):
    L, D, N = ada_w.shape
    B = c.shape[0]
    bp = -(-B // SUBLANES) * SUBLANES
    c_pad = jnp.zeros((bp, D), F32).at[:B].set(c)
    tn = _tile(N, 512)
    out = pl.pallas_call(
        _adaln_kernel,
        grid=(L, N // tn),
        in_specs=[
            pl.BlockSpec((bp, D), lambda l, j: (0, 0)),
            pl.BlockSpec((None, D, tn), lambda l, j: (l, 0, j)),
            pl.BlockSpec((None, 1, tn), lambda l, j: (l, 0, j)),
        ],
        out_specs=pl.BlockSpec((None, bp, tn), lambda l, j: (l, 0, j)),
        out_shape=jax.ShapeDtypeStruct((L, bp, N), F32),
        compiler_params=_params(VMEM_LIMIT_MID, ("parallel", "parallel")),
        name="adaln",
    )(c_pad, ada_w, ada_b.reshape(L, 1, N))
    return out[:, :B]


def _rms(x):
    return x * lax.rsqrt(jnp.mean(x * x, axis=-1, keepdims=True) + EPS)


def _norm_mod_kernel(x_ref, g_ref, shift_ref, scale_ref, o_ref):
    y = _rms(x_ref[...]) * g_ref[...]
    o_ref[...] = (y * (1.0 + scale_ref[...]) + shift_ref[...]).astype(o_ref.dtype)


def _norm_mod(x, g, mod3):
    B, S, D = x.shape
    ts = _tile(S, 256)
    return pl.pallas_call(
        _norm_mod_kernel,
        grid=(B, S // ts),
        in_specs=[
            pl.BlockSpec((None, ts, D), lambda b, i: (b, i, 0)),
            pl.BlockSpec((1, D), lambda b, i: (0, 0)),
            pl.BlockSpec((None, 1, D), lambda b, i: (b, 0, 0)),
            pl.BlockSpec((None, 1, D), lambda b, i: (b, 0, 1)),
        ],
        out_specs=pl.BlockSpec((None, ts, D), lambda b, i: (b, i, 0)),
        out_shape=jax.ShapeDtypeStruct((B, S, D), BF16),
        compiler_params=_params(VMEM_LIMIT_MID, ("parallel", "parallel")),
        name="norm_mod",
    )(x, g.reshape(1, D), mod3, mod3)


def _norm_dual_kernel(x_ref, gkv_ref, g_ref, shift_ref, scale_ref, okv_ref, o_ref):
    y = _rms(x_ref[...])
    okv_ref[...] = (y * gkv_ref[...]).astype(okv_ref.dtype)
    o_ref[...] = ((y * g_ref[...]) * (1.0 + scale_ref[...]) + shift_ref[...]).astype(o_ref.dtype)


def _norm_dual(x, g_kv, g, mod3):
    B, S, D = x.shape
    ts = _tile(S, 256)
    row = pl.BlockSpec((None, ts, D), lambda b, i: (b, i, 0))
    vec = pl.BlockSpec((1, D), lambda b, i: (0, 0))
    return pl.pallas_call(
        _norm_dual_kernel,
        grid=(B, S // ts),
        in_specs=[row, vec, vec,
                  pl.BlockSpec((None, 1, D), lambda b, i: (b, 0, 0)),
                  pl.BlockSpec((None, 1, D), lambda b, i: (b, 0, 1))],
        out_specs=[row, row],
        out_shape=[jax.ShapeDtypeStruct((B, S, D), BF16)] * 2,
        compiler_params=_params(VMEM_LIMIT_MID, ("parallel", "parallel")),
        name="norm_dual",
    )(x, g_kv.reshape(1, D), g.reshape(1, D), mod3, mod3)


def _final_norm_kernel(x_ref, g_ref, o_ref):
    o_ref[...] = _rms(x_ref[...]) * g_ref[...]


def _final_norm(x, g):
    B, S, D = x.shape
    ts = _tile(S, 256)
    row = pl.BlockSpec((None, ts, D), lambda b, i: (b, i, 0))
    return pl.pallas_call(
        _final_norm_kernel,
        grid=(B, S // ts),
        in_specs=[row, pl.BlockSpec((1, D), lambda b, i: (0, 0))],
        out_specs=row,
        out_shape=jax.ShapeDtypeStruct((B, S, D), F32),
        compiler_params=_params(VMEM_LIMIT_MID, ("parallel", "parallel")),
        name="final_norm",
    )(x, g.reshape(1, D))


def _rope_table_kernel(inv_ref, cos_ref, sin_ref):
    ts = cos_ref.shape[0]
    pos = (pl.program_id(0) * ts + lax.broadcasted_iota(jnp.int32, (ts, HEAD_DIM), 0)).astype(F32)
    ang = pos * inv_ref[...]
    lane = lax.broadcasted_iota(jnp.int32, (ts, HEAD_DIM), 1)
    cos_ref[...] = jnp.cos(ang)
    sin_ref[...] = jnp.where(lane < HEAD_DIM // 2, -jnp.sin(ang), jnp.sin(ang))


def _rope_tables(S):
    half = HEAD_DIM // 2
    inv = ROPE_THETA ** (-jnp.arange(half, dtype=F32) / half)
    inv2 = jnp.concatenate([inv, inv]).reshape(1, HEAD_DIM)
    ts = _tile(S, 512)
    tab = pl.BlockSpec((ts, HEAD_DIM), lambda i: (i, 0))
    return pl.pallas_call(
        _rope_table_kernel,
        grid=(S // ts,),
        in_specs=[pl.BlockSpec((1, HEAD_DIM), lambda i: (0, 0))],
        out_specs=[tab, tab],
        out_shape=[jax.ShapeDtypeStruct((S, HEAD_DIM), F32)] * 2,
        name="rope_tables",
    )(inv2)


def _proj_kernel(*refs, n_lead, lead_scale, rope):
    if rope:
        x_ref, w_ref, cos_ref, sin_ref, o_ref = refs
    else:
        x_ref, w_ref, o_ref = refs
    acc = jnp.dot(x_ref[...], w_ref[...], preferred_element_type=F32)
    tn = acc.shape[1]
    j = pl.program_id(1)

    def lead():
        if rope:
            cos, sin = cos_ref[...], sin_ref[...]
            for h in range(tn // HEAD_DIM):
                sl = slice(h * HEAD_DIM, (h + 1) * HEAD_DIM)
                xh = acc[:, sl]
                r = xh * cos + pltpu.roll(xh, HEAD_DIM // 2, axis=1) * sin
                o_ref[:, sl] = (r * lead_scale).astype(o_ref.dtype)
        else:
            o_ref[...] = (acc * lead_scale).astype(o_ref.dtype)

    def rest():
        o_ref[...] = acc.astype(o_ref.dtype)

    if n_lead == 0:
        rest()
    else:
        pl.when(j < n_lead)(lead)
        pl.when(j >= n_lead)(rest)


def _proj(x, w, *, lead_cols=0, lead_scale=1.0, rope_tabs=None, seq=None, tn_pref=1024):
    M, K = x.shape
    N = w.shape[1]
    tm = _tile(M if seq is None else seq, 1024)
    tn = _tile(N, tn_pref)
    assert lead_cols % tn == 0 and tn % HEAD_DIM == 0
    in_specs = [pl.BlockSpec((tm, K), lambda i, j: (i, 0)),
                pl.BlockSpec((K, tn), lambda i, j: (0, j))]
    args = [x, w]
    if rope_tabs is not None:
        spt = seq // tm
        tab = pl.BlockSpec((tm, HEAD_DIM), lambda i, j: (i % spt, 0))
        in_specs += [tab, tab]
        args += list(rope_tabs)
    return pl.pallas_call(
        functools.partial(_proj_kernel, n_lead=lead_cols // tn, lead_scale=lead_scale,
                          rope=rope_tabs is not None),
        grid=(M // tm, N // tn),
        in_specs=in_specs,
        out_specs=pl.BlockSpec((tm, tn), lambda i, j: (i, j)),
        out_shape=jax.ShapeDtypeStruct((M, N), BF16),
        compiler_params=_params(VMEM_LIMIT_BIG, ("parallel", "parallel")),
        name="proj",
    )(*args)


def _out_proj_kernel(a_ref, w_ref, x_ref, gate_ref, o_ref):
    y = jnp.dot(a_ref[...], w_ref[...], preferred_element_type=F32)
    o_ref[...] = x_ref[...] + gate_ref[...] * y


def _out_proj(a, w, x, mod3):
    B, S, N = x.shape
    K = a.shape[1]
    tm = _tile(S, 1024)
    tn = _tile(N, 512)
    spt = S // tm
    nj = N // tn
    out = pl.pallas_call(
        _out_proj_kernel,
        grid=(B * spt, nj),
        in_specs=[
            pl.BlockSpec((tm, K), lambda i, j: (i, 0)),
            pl.BlockSpec((K, tn), lambda i, j: (0, j)),
            pl.BlockSpec((tm, tn), lambda i, j: (i, j)),
            pl.BlockSpec((None, 1, tn), lambda i, j: (i // spt, 0, 2 * nj + j)),
        ],
        out_specs=pl.BlockSpec((tm, tn), lambda i, j: (i, j)),
        out_shape=jax.ShapeDtypeStruct((B * S, N), F32),
        compiler_params=_params(VMEM_LIMIT_BIG, ("parallel", "parallel")),
        name="out_proj",
    )(a, w, x.reshape(B * S, N), mod3)
    return out.reshape(B, S, N)


def _split3(x):
    hi = x.astype(BF16)
    r1 = x - hi.astype(F32)
    mid = r1.astype(BF16)
    lo = (r1 - mid.astype(F32)).astype(BF16)
    return hi, mid, lo


def _forget_kernel(h_ref, w_ref, b_ref, aq_ref, ak_ref, carry_ref, *, n_heads):
    ts = h_ref.shape[0]

    @pl.when(pl.program_id(1) == 0)
    def _():
        carry_ref[...] = jnp.zeros_like(carry_ref)

    f = jnp.dot(h_ref[...], w_ref[...], preferred_element_type=F32) + b_ref[...]
    log_f = jnp.minimum(f, 0.0) - jnp.log1p(jnp.exp(-jnp.abs(f)))
    row = lax.broadcasted_iota(jnp.int32, (ts, ts), 0)
    col = lax.broadcasted_iota(jnp.int32, (ts, ts), 1)
    tri = jnp.where(row >= col, 1.0, 0.0).astype(BF16)
    cum = carry_ref[...]
    for part in _split3(log_f):
        cum = cum + jnp.dot(tri, part, preferred_element_type=F32)
    carry_ref[...] = cum[ts - 1:ts, :]

    hi, mid, lo = (p.astype(F32) for p in _split3(cum))
    lane = lax.broadcasted_iota(jnp.int32, (ts, LANES), 1)
    ones_q = jnp.where((lane >= 3) & (lane < 6), 1.0, 0.0)
    ones_k = jnp.where(lane < 3, 1.0, 0.0)
    for h in range(n_heads):
        c = slice(h, h + 1)
        fq = jnp.where(lane == 0, hi[:, c], jnp.where(lane == 1, mid[:, c],
                       jnp.where(lane == 2, lo[:, c], ones_q)))
        fk = jnp.where(lane == 3, -hi[:, c], jnp.where(lane == 4, -mid[:, c],
                       jnp.where(lane == 5, -lo[:, c], ones_k)))
        aq_ref[h] = fq.astype(BF16)
        ak_ref[h] = fk.astype(BF16)


def _forget_columns(h, w_f, b_f):
    B, S, D = h.shape
    H = w_f.shape[1]
    assert H <= LANES
    w_pad = jnp.zeros((D, LANES), BF16).at[:, :H].set(w_f.astype(BF16))
    b_pad = jnp.zeros((1, LANES), F32).at[0, :H].set(b_f.astype(F32))
    ts = _tile(S, 256)
    out = pl.BlockSpec((None, H, ts, LANES), lambda b, i: (b, 0, i, 0))
    return pl.pallas_call(
        functools.partial(_forget_kernel, n_heads=H),
        grid=(B, S // ts),
        in_specs=[
            pl.BlockSpec((None, ts, D), lambda b, i: (b, i, 0)),
            pl.BlockSpec((D, LANES), lambda b, i: (0, 0)),
            pl.BlockSpec((1, LANES), lambda b, i: (0, 0)),
        ],
        out_specs=[out, out],
        out_shape=[jax.ShapeDtypeStruct((B, H, S, LANES), BF16)] * 2,
        scratch_shapes=[pltpu.VMEM((1, LANES), F32)],
        compiler_params=_params(VMEM_LIMIT_MID, ("parallel", "arbitrary")),
        name="forget_columns",
    )(h, w_pad, b_pad)


def _fox_kernel(q_ref, k_ref, v_ref, z_ref, aq_ref, ak_ref, o_ref, *, tq):
    S = q_ref.shape[0]
    nq = S // tq
    row = lax.broadcasted_iota(jnp.int32, (tq, tq), 0)
    col = lax.broadcasted_iota(jnp.int32, (tq, tq), 1)
    causal = row >= col

    def step(qa, ks, carry, masked):
        m, l, acc = carry
        ka = jnp.concatenate([k_ref[pl.ds(ks, tq), :], ak_ref[pl.ds(ks, tq), :]], axis=1)
        s = lax.dot_general(qa, ka, (((1,), (1,)), ((), ())), preferred_element_type=F32)
        if masked:
            s = jnp.where(causal, s, -jnp.inf)
        m_new = jnp.maximum(m, jnp.max(s, axis=1, keepdims=True))
        alpha = jnp.exp(m - m_new)
        p = jnp.exp(s - m_new)
        l = alpha * l + jnp.sum(p, axis=1, keepdims=True)
        acc = alpha * acc + jnp.dot(p.astype(BF16), v_ref[pl.ds(ks, tq), :],
                                    preferred_element_type=F32)
        return m_new, l, acc

    for i in range(nq):
        qs = i * tq
        qa = jnp.concatenate([q_ref[qs:qs + tq, :], aq_ref[qs:qs + tq, :]], axis=1)
        carry = (jnp.full((tq, 1), -jnp.inf, F32), jnp.zeros((tq, 1), F32),
                 jnp.zeros((tq, HEAD_DIM), F32))
        if i > 0:
            carry = lax.fori_loop(
                0, i, lambda j, c: step(qa, pl.multiple_of(j * tq, tq), c, False), carry)
        m, l, acc = step(qa, qs, carry, True)
        z = z_ref[qs:qs + tq, :].astype(F32)
        o_ref[qs:qs + tq, :] = ((acc / l) * (z * jax.nn.sigmoid(z))).astype(o_ref.dtype)


def _fox_attention(qkvz, aq, ak, n_heads):
    B, S, _ = qkvz.shape
    H = n_heads
    tq = _tile(S, 512)
    col = lambda c: pl.BlockSpec((None, S, HEAD_DIM), lambda b, h: (b, 0, c * H + h))
    side = pl.BlockSpec((None, None, S, LANES), lambda b, h: (b, h, 0, 0))
    return pl.pallas_call(
        functools.partial(_fox_kernel, tq=tq),
        grid=(B, H),
        in_specs=[col(0), col(1), col(2), col(3), side, side],
        out_specs=pl.BlockSpec((None, S, HEAD_DIM), lambda b, h: (b, 0, h)),
        out_shape=jax.ShapeDtypeStruct((B, S, H * HEAD_DIM), BF16),
        compiler_params=_params(VMEM_LIMIT_MID, ("parallel", "parallel")),
        name="fox_attention",
    )(qkvz, qkvz, qkvz, qkvz, aq, ak)


def _swa_kernel(q_ref, z_ref, k_ref, v_ref, sink_ref, o_ref, *, group):
    tq = q_ref.shape[0]
    blk = WINDOW
    rows = group * blk
    base = pl.program_id(2) * tq
    qi = lax.broadcasted_iota(jnp.int32, (rows, 2 * blk), 0) & (blk - 1)
    kj = lax.broadcasted_iota(jnp.int32, (rows, 2 * blk), 1)
    sink = jnp.concatenate(
        [jnp.broadcast_to(sink_ref[:, g * LANES:g * LANES + 1], (blk, 1)) for g in range(group)],
        axis=0)
    for nb in range(tq // blk):
        start = base + nb * blk
        kstart = pl.multiple_of(jnp.maximum(start - blk, 0), blk)
        kb = k_ref[pl.ds(kstart, 2 * blk), :]
        vb = v_ref[pl.ds(kstart, 2 * blk), :]
        qt = q_ref[nb * blk:(nb + 1) * blk, :]
        qs = jnp.concatenate([qt[:, g * HEAD_DIM:(g + 1) * HEAD_DIM] for g in range(group)], axis=0)
        s = lax.dot_general(qs, kb, (((1,), (1,)), ((), ())), preferred_element_type=F32)
        diff = (start + qi) - (kstart + kj)
        s = jnp.where((diff >= 0) & (diff < WINDOW), s, -jnp.inf)
        m = jnp.maximum(jnp.max(s, axis=1, keepdims=True), sink)
        p = jnp.exp(s - m)
        denom = jnp.sum(p, axis=1, keepdims=True) + jnp.exp(sink - m)
        o = jnp.dot(p.astype(BF16), vb, preferred_element_type=F32) / denom
        for g in range(group):
            sl = slice(g * HEAD_DIM, (g + 1) * HEAD_DIM)
            z = z_ref[nb * blk:(nb + 1) * blk, sl].astype(F32)
            o_ref[nb * blk:(nb + 1) * blk, sl] = (
                o[g * blk:(g + 1) * blk, :] * (z * jax.nn.sigmoid(z))).astype(o_ref.dtype)


def _swa_attention(qz, kv, sinks):
    B, S, W2 = qz.shape
    width = W2 // 2
    hq = width // HEAD_DIM
    group = hq // KV_HEADS
    gw = group * HEAD_DIM
    tq = _tile(S, 512)
    sink_b = jnp.broadcast_to(sinks.astype(F32).reshape(KV_HEADS, 1, group, 1),
                              (KV_HEADS, 1, group, LANES)).reshape(KV_HEADS, 1, group * LANES)
    return pl.pallas_call(
        functools.partial(_swa_kernel, group=group),
        grid=(B, KV_HEADS, S // tq),
        in_specs=[
            pl.BlockSpec((None, tq, gw), lambda b, g, i: (b, i, g)),
            pl.BlockSpec((None, tq, gw), lambda b, g, i: (b, i, KV_HEADS + g)),
            pl.BlockSpec((None, S, HEAD_DIM), lambda b, g, i: (b, 0, g)),
            pl.BlockSpec((None, S, HEAD_DIM), lambda b, g, i: (b, 0, KV_HEADS + g)),
            pl.BlockSpec((None, 1, group * LANES), lambda b, g, i: (g, 0, 0)),
        ],
        out_specs=pl.BlockSpec((None, tq, gw), lambda b, g, i: (b, i, g)),
        out_shape=jax.ShapeDtypeStruct((B, S, width), BF16),
        compiler_params=_params(VMEM_LIMIT_MID, ("parallel", "parallel", "parallel")),
        name="swa_attention",
    )(qz, qz, kv, kv, sink_b)


def kernel(x, c, norm_g, ada_w, ada_b, a_w_in, a_b_f, a_w_out, kv_norm_g, kv_w,
           b_w_in, b_sinks, b_w_out, final_norm_g):
    B, S, D = x.shape
    n_a = a_w_in.shape[0]
    n_b = b_w_in.shape[0]
    depth = n_a + n_b
    sm_scale = HEAD_DIM ** -0.5

    mod = _adaln(c, ada_w, ada_b)
    rope_tabs = _rope_tables(S)

    h = x
    kv = None
    for layer in range(depth):
        mod3 = mod[layer].reshape(B, 1, 3 * D)
        if layer < n_a:
            a_heads = a_b_f.shape[1]
            a_width = a_heads * HEAD_DIM
            hn = _norm_mod(h, norm_g[layer], mod3)
            w_in = a_w_in[layer]
            qkvz = _proj(hn.reshape(B * S, D), w_in[:, :4 * a_width].astype(BF16),
                         lead_cols=a_width, lead_scale=sm_scale)
            aq, ak = _forget_columns(hn, w_in[:, 4 * a_width:], a_b_f[layer])
            o = _fox_attention(qkvz.reshape(B, S, 4 * a_width), aq, ak, a_heads)
            h = _out_proj(o.reshape(B * S, a_width), a_w_out[layer].astype(BF16), h, mod3)
        else:
            j = layer - n_a
            if kv is None:
                hkv, hn = _norm_dual(h, kv_norm_g, norm_g[layer], mod3)
                kvw = kv_w.shape[1] // 2
                kv = _proj(hkv.reshape(B * S, D), kv_w.astype(BF16), lead_cols=kvw,
                           rope_tabs=rope_tabs, seq=S, tn_pref=kvw).reshape(B, S, 2 * kvw)
            else:
                hn = _norm_mod(h, norm_g[layer], mod3)
            b_width = b_w_in.shape[2] // 2
            qz = _proj(hn.reshape(B * S, D), b_w_in[j].astype(BF16), lead_cols=b_width,
                       lead_scale=sm_scale, rope_tabs=rope_tabs, seq=S)
            o = _swa_attention(qz.reshape(B, S, 2 * b_width), kv, b_sinks[j])
            h = _out_proj(o.reshape(B * S, b_width), b_w_out[j].astype(BF16), h, mod3)
    return _final_norm(h, final_norm_g)
```

```python
import functools
import math

import jax
import jax.numpy as jnp
from jax import lax
from jax.experimental import pallas as pl
from jax.experimental.pallas import tpu as pltpu

HEAD_DIM = 128
KV_HEADS = 4
WINDOW = 128
ROPE_THETA = 10000.0
EPS = 1e-6

V7X_VMEM_BYTES = 64 * 1024 * 1024
VMEM_LIMIT_BIG = 56 * 1024 * 1024
VMEM_LIMIT_MID = 40 * 1024 * 1024
LANES = 128
SUBLANES = 8
BF16_SUBLANES = 16
VT_ROWS = HEAD_DIM + BF16_SUBLANES
LOG2E = math.log2(math.e)

F32 = jnp.float32
BF16 = jnp.bfloat16


def _params(limit, sem):
    return pltpu.CompilerParams(dimension_semantics=sem, vmem_limit_bytes=limit)


def _tile(n, pref):
    t = min(n, pref)
    assert n % t == 0, (n, t)
    return t


def _adaln_kernel(c_ref, w_ref, b_ref, o_ref):
    c = c_ref[...]
    sc = (c * jax.nn.sigmoid(c)).astype(BF16)
    acc = jnp.dot(sc, w_ref[...].astype(BF16), preferred_element_type=F32)
    o_ref[...] = acc + b_ref[...]


def _adaln(c, ada_w, ada_b):
    L, D, N = ada_w.shape
    B = c.shape[0]
    bp = -(-B // SUBLANES) * SUBLANES
    c_pad = jnp.zeros((bp, D), F32).at[:B].set(c)
    tn = _tile(N, 512)
    out = pl.pallas_call(
        _adaln_kernel,
        grid=(L, N // tn),
        in_specs=[
            pl.BlockSpec((bp, D), lambda l, j: (0, 0)),
            pl.BlockSpec((None, D, tn), lambda l, j: (l, 0, j)),
            pl.BlockSpec((None, 1, tn), lambda l, j: (l, 0, j)),
        ],
        out_specs=pl.BlockSpec((None, bp, tn), lambda l, j: (l, 0, j)),
        out_shape=jax.ShapeDtypeStruct((L, bp, N), F32),
        compiler_params=_params(VMEM_LIMIT_MID, ("parallel", "parallel")),
        name="adaln",
    )(c_pad, ada_w, ada_b.reshape(L, 1, N))
    return out[:, :B]


def _rms(x):
    return x * lax.rsqrt(jnp.mean(x * x, axis=-1, keepdims=True) + EPS)


def _norm_mod_kernel(x_ref, g_ref, shift_ref, scale_ref, o_ref):
    y = _rms(x_ref[...]) * g_ref[...]
    o_ref[...] = (y * (1.0 + scale_ref[...]) + shift_ref[...]).astype(o_ref.dtype)


def _norm_mod(x, g, mod3):
    B, S, D = x.shape
    ts = _tile(S, 256)
    return pl.pallas_call(
        _norm_mod_kernel,
        grid=(B, S // ts),
        in_specs=[
            pl.BlockSpec((None, ts, D), lambda b, i: (b, i, 0)),
            pl.BlockSpec((1, D), lambda b, i: (0, 0)),
            pl.BlockSpec((None, 1, D), lambda b, i: (b, 0, 0)),
            pl.BlockSpec((None, 1, D), lambda b, i: (b, 0, 1)),
        ],
        out_specs=pl.BlockSpec((None, ts, D), lambda b, i: (b, i, 0)),
        out_shape=jax.ShapeDtypeStruct((B, S, D), BF16),
        compiler_params=_params(VMEM_LIMIT_MID, ("parallel", "parallel")),
        name="norm_mod",
    )(x, g.reshape(1, D), mod3, mod3)


def _norm_dual_kernel(x_ref, gkv_ref, g_ref, shift_ref, scale_ref, okv_ref, o_ref):
    y = _rms(x_ref[...])
    okv_ref[...] = (y * gkv_ref[...]).astype(okv_ref.dtype)
    o_ref[...] = ((y * g_ref[...]) * (1.0 + scale_ref[...]) + shift_ref[...]).astype(o_ref.dtype)


def _norm_dual(x, g_kv, g, mod3):
    B, S, D = x.shape
    ts = _tile(S, 256)
    row = pl.BlockSpec((None, ts, D), lambda b, i: (b, i, 0))
    vec = pl.BlockSpec((1, D), lambda b, i: (0, 0))
    return pl.pallas_call(
        _norm_dual_kernel,
        grid=(B, S // ts),
        in_specs=[row, vec, vec,
                  pl.BlockSpec((None, 1, D), lambda b, i: (b, 0, 0)),
                  pl.BlockSpec((None, 1, D), lambda b, i: (b, 0, 1))],
        out_specs=[row, row],
        out_shape=[jax.ShapeDtypeStruct((B, S, D), BF16)] * 2,
        compiler_params=_params(VMEM_LIMIT_MID, ("parallel", "parallel")),
        name="norm_dual",
    )(x, g_kv.reshape(1, D), g.reshape(1, D), mod3, mod3)


def _final_norm_kernel(x_ref, g_ref, o_ref):
    o_ref[...] = _rms(x_ref[...]) * g_ref[...]


def _final_norm(x, g):
    B, S, D = x.shape
    ts = _tile(S, 256)
    row = pl.BlockSpec((None, ts, D), lambda b, i: (b, i, 0))
    return pl.pallas_call(
        _final_norm_kernel,
        grid=(B, S // ts),
        in_specs=[row, pl.BlockSpec((1, D), lambda b, i: (0, 0))],
        out_specs=row,
        out_shape=jax.ShapeDtypeStruct((B, S, D), F32),
        compiler_params=_params(VMEM_LIMIT_MID, ("parallel", "parallel")),
        name="final_norm",
    )(x, g.reshape(1, D))


def _rope_table_kernel(inv_ref, cos_ref, sin_ref):
    ts = cos_ref.shape[0]
    pos = (pl.program_id(0) * ts + lax.broadcasted_iota(jnp.int32, (ts, HEAD_DIM), 0)).astype(F32)
    ang = pos * inv_ref[...]
    lane = lax.broadcasted_iota(jnp.int32, (ts, HEAD_DIM), 1)
    cos_ref[...] = jnp.cos(ang)
    sin_ref[...] = jnp.where(lane < HEAD_DIM // 2, -jnp.sin(ang), jnp.sin(ang))


def _rope_tables(S):
    half = HEAD_DIM // 2
    inv = ROPE_THETA ** (-jnp.arange(half, dtype=F32) / half)
    inv2 = jnp.concatenate([inv, inv]).reshape(1, HEAD_DIM)
    ts = _tile(S, 512)
    tab = pl.BlockSpec((ts, HEAD_DIM), lambda i: (i, 0))
    return pl.pallas_call(
        _rope_table_kernel,
        grid=(S // ts,),
        in_specs=[pl.BlockSpec((1, HEAD_DIM), lambda i: (0, 0))],
        out_specs=[tab, tab],
        out_shape=[jax.ShapeDtypeStruct((S, HEAD_DIM), F32)] * 2,
        name="rope_tables",
    )(inv2)


def _proj_kernel(*refs, n_lead, lead_scale, rope):
    if rope:
        x_ref, w_ref, cos_ref, sin_ref, o_ref = refs
    else:
        x_ref, w_ref, o_ref = refs
    acc = jnp.dot(x_ref[...], w_ref[...], preferred_element_type=F32)
    tn = acc.shape[1]
    j = pl.program_id(1)

    def lead():
        if rope:
            cos, sin = cos_ref[...], sin_ref[...]
            for h in range(tn // HEAD_DIM):
                sl = slice(h * HEAD_DIM, (h + 1) * HEAD_DIM)
                xh = acc[:, sl]
                r = xh * cos + pltpu.roll(xh, HEAD_DIM // 2, axis=1) * sin
                o_ref[:, sl] = (r * lead_scale).astype(o_ref.dtype)
        else:
            o_ref[...] = (acc * lead_scale).astype(o_ref.dtype)

    def rest():
        o_ref[...] = acc.astype(o_ref.dtype)

    if n_lead == 0:
        rest()
    else:
        pl.when(j < n_lead)(lead)
        pl.when(j >= n_lead)(rest)


def _proj(x, w, *, n_cols=None, lead_cols=0, lead_scale=1.0, rope_tabs=None, seq=None,
          tn_pref=1024):
    M, K = x.shape
    N = w.shape[1] if n_cols is None else n_cols
    tm = _tile(M if seq is None else seq, 1024)
    tn = _tile(N, tn_pref)
    assert lead_cols % tn == 0 and tn % HEAD_DIM == 0
    in_specs = [pl.BlockSpec((tm, K), lambda i, j: (i, 0)),
                pl.BlockSpec((K, tn), lambda i, j: (0, j))]
    args = [x, w]
    if rope_tabs is not None:
        spt = seq // tm
        tab = pl.BlockSpec((tm, HEAD_DIM), lambda i, j: (i % spt, 0))
        in_specs += [tab, tab]
        args += list(rope_tabs)
    return pl.pallas_call(
        functools.partial(_proj_kernel, n_lead=lead_cols // tn, lead_scale=lead_scale,
                          rope=rope_tabs is not None),
        grid=(M // tm, N // tn),
        in_specs=in_specs,
        out_specs=pl.BlockSpec((tm, tn), lambda i, j: (i, j)),
        out_shape=jax.ShapeDtypeStruct((M, N), BF16),
        compiler_params=_params(VMEM_LIMIT_BIG, ("parallel", "parallel")),
        name="proj",
    )(*args)


def _out_proj_kernel(a_ref, w_ref, x_ref, gate_ref, o_ref):
    y = jnp.dot(a_ref[...], w_ref[...], preferred_element_type=F32)
    o_ref[...] = x_ref[...] + gate_ref[...] * y


def _out_proj(a, w, x, mod3):
    B, S, N = x.shape
    K = a.shape[1]
    tm = _tile(S, 1024)
    tn = _tile(N, 512)
    spt = S // tm
    nj = N // tn
    out = pl.pallas_call(
        _out_proj_kernel,
        grid=(B * spt, nj),
        in_specs=[
            pl.BlockSpec((tm, K), lambda i, j: (i, 0)),
            pl.BlockSpec((K, tn), lambda i, j: (0, j)),
            pl.BlockSpec((tm, tn), lambda i, j: (i, j)),
            pl.BlockSpec((None, 1, tn), lambda i, j: (i // spt, 0, 2 * nj + j)),
        ],
        out_specs=pl.BlockSpec((tm, tn), lambda i, j: (i, j)),
        out_shape=jax.ShapeDtypeStruct((B * S, N), F32),
        compiler_params=_params(VMEM_LIMIT_BIG, ("parallel", "parallel")),
        name="out_proj",
    )(a, w, x.reshape(B * S, N), mod3)
    return out.reshape(B, S, N)


def _split3(x):
    hi = x.astype(BF16)
    r1 = x - hi.astype(F32)
    mid = r1.astype(BF16)
    lo = (r1 - mid.astype(F32)).astype(BF16)
    return hi, mid, lo


def _forget_kernel(h_ref, w_ref, b_ref, aq_ref, ak_ref, carry_ref, *, n_heads):
    ts = h_ref.shape[0]

    @pl.when(pl.program_id(1) == 0)
    def _():
        carry_ref[...] = jnp.zeros_like(carry_ref)

    f = jnp.dot(h_ref[...], w_ref[...], preferred_element_type=F32) + b_ref[...]
    log_f = jnp.minimum(f, 0.0) - jnp.log1p(jnp.exp(-jnp.abs(f)))
    row = lax.broadcasted_iota(jnp.int32, (ts, ts), 0)
    col = lax.broadcasted_iota(jnp.int32, (ts, ts), 1)
    tri = jnp.where(row >= col, 1.0, 0.0).astype(BF16)
    cum = carry_ref[...]
    for part in _split3(log_f):
        cum = cum + jnp.dot(tri, part, preferred_element_type=F32)
    carry_ref[...] = cum[ts - 1:ts, :]

    hi, mid, lo = (p.astype(F32) for p in _split3(cum * LOG2E))
    lane = lax.broadcasted_iota(jnp.int32, (ts, LANES), 1)
    ones_q = jnp.where((lane >= 3) & (lane < 6), 1.0, 0.0)
    ones_k = jnp.where(lane < 3, 1.0, 0.0)
    for h in range(n_heads):
        c = slice(h, h + 1)
        fq = jnp.where(lane == 0, hi[:, c], jnp.where(lane == 1, mid[:, c],
                       jnp.where(lane == 2, lo[:, c], ones_q)))
        fk = jnp.where(lane == 3, -hi[:, c], jnp.where(lane == 4, -mid[:, c],
                       jnp.where(lane == 5, -lo[:, c], ones_k)))
        aq_ref[h] = fq.astype(BF16)
        ak_ref[h] = fk.astype(BF16)


def _forget_columns(h, w_f, b_f):
    B, S, D = h.shape
    H = w_f.shape[1]
    assert H <= LANES
    w_pad = jnp.zeros((D, LANES), BF16).at[:, :H].set(w_f.astype(BF16))
    b_pad = jnp.zeros((1, LANES), F32).at[0, :H].set(b_f.astype(F32))
    ts = _tile(S, 256)
    out = pl.BlockSpec((None, H, ts, LANES), lambda b, i: (b, 0, i, 0))
    return pl.pallas_call(
        functools.partial(_forget_kernel, n_heads=H),
        grid=(B, S // ts),
        in_specs=[
            pl.BlockSpec((None, ts, D), lambda b, i: (b, i, 0)),
            pl.BlockSpec((D, LANES), lambda b, i: (0, 0)),
            pl.BlockSpec((1, LANES), lambda b, i: (0, 0)),
        ],
        out_specs=[out, out],
        out_shape=[jax.ShapeDtypeStruct((B, H, S, LANES), BF16)] * 2,
        scratch_shapes=[pltpu.VMEM((1, LANES), F32)],
        compiler_params=_params(VMEM_LIMIT_MID, ("parallel", "arbitrary")),
        name="forget_columns",
    )(h, w_pad, b_pad)


def _fox_kernel(q_ref, k_ref, v_ref, z_ref, aq_ref, ak_ref, o_ref, vt_ref, *, tq, hp):
    S = q_ref.shape[0]
    nq = S // tq
    hd = HEAD_DIM
    ones = jnp.ones((VT_ROWS - hd, tq), BF16)
    for g in range(hp):
        for j in range(nq):
            vblk = v_ref[j * tq:(j + 1) * tq, g * hd:(g + 1) * hd].astype(F32)
            vt_ref[g, j, 0:hd, :] = vblk.T.astype(BF16)
            vt_ref[g, j, hd:VT_ROWS, :] = ones

    key = lax.broadcasted_iota(jnp.int32, (tq, tq), 0)
    qry = lax.broadcasted_iota(jnp.int32, (tq, tq), 1)
    causal_t = key <= qry

    def scores(g, qa, ks, masked):
        ka = jnp.concatenate([k_ref[pl.ds(ks, tq), g * hd:(g + 1) * hd],
                              ak_ref[g, pl.ds(ks, tq), :]], axis=1)
        s_t = lax.dot_general(ka, qa, (((1,), (1,)), ((), ())), preferred_element_type=F32)
        return jnp.where(causal_t, s_t, -jnp.inf) if masked else s_t

    def update(g, j, s_t, carry):
        m, acc = carry
        m_new = jnp.maximum(m, jnp.max(s_t, axis=0, keepdims=True))
        alpha = jnp.exp2(m - m_new)
        p_t = jnp.exp2(s_t - m_new).astype(BF16)
        acc = alpha * acc + jnp.dot(vt_ref[g, j], p_t, preferred_element_type=F32)
        return m_new, acc

    def steps(qas, j, ks, carry, masked):
        s_ts = [scores(g, qas[g], ks, masked) for g in range(hp)]
        return tuple(update(g, j, s_ts[g], carry[g]) for g in range(hp))

    for i in range(nq):
        qs = i * tq
        qas = [jnp.concatenate([q_ref[qs:qs + tq, g * hd:(g + 1) * hd], aq_ref[g, qs:qs + tq, :]],
                               axis=1) for g in range(hp)]
        carry = tuple((jnp.full((1, tq), -jnp.inf, F32), jnp.zeros((VT_ROWS, tq), F32))
                      for _ in range(hp))
        s_cur = [scores(g, qas[g], 0, i == 0) for g in range(hp)]
        for j in range(i + 1):
            if j < i:
                s_nxt = [scores(g, qas[g], (j + 1) * tq, j + 1 == i) for g in range(hp)]
            carry = tuple(update(g, j, s_cur[g], carry[g]) for g in range(hp))
            if j < i:
                s_cur = s_nxt
        for g in range(hp):
            acc = carry[g][1]
            o = (acc[0:hd, :] * (1.0 / acc[hd:hd + 1, :])).T
            z = z_ref[qs:qs + tq, g * hd:(g + 1) * hd].astype(F32)
            o_ref[qs:qs + tq, g * hd:(g + 1) * hd] = (
                o * (z * jax.nn.sigmoid(z))).astype(o_ref.dtype)


def _fox_attention(qkvz, aq, ak, n_heads):
    B, S, _ = qkvz.shape
    H = n_heads
    hp = 1
    tq = _tile(S, 512)
    w = hp * HEAD_DIM
    col = lambda c: pl.BlockSpec((None, S, w), lambda b, h: (b, 0, c * (H // hp) + h))
    side = pl.BlockSpec((None, hp, S, LANES), lambda b, h: (b, h, 0, 0))
    return pl.pallas_call(
        functools.partial(_fox_kernel, tq=tq, hp=hp),
        grid=(B, H // hp),
        in_specs=[col(0), col(1), col(2), col(3), side, side],
        out_specs=pl.BlockSpec((None, S, w), lambda b, h: (b, 0, h)),
        out_shape=jax.ShapeDtypeStruct((B, S, H * HEAD_DIM), BF16),
        scratch_shapes=[pltpu.VMEM((hp, S // tq, VT_ROWS, tq), BF16)],
        compiler_params=_params(VMEM_LIMIT_BIG, ("parallel", "parallel")),
        name="fox_attention",
    )(qkvz, qkvz, qkvz, qkvz, aq, ak)


def _swa_kernel(q_ref, z_ref, k_ref, v_ref, sink_ref, o_ref, *, group):
    tq = q_ref.shape[0]
    blk = WINDOW
    hd = HEAD_DIM
    base = pl.program_id(2) * tq
    kk = lax.broadcasted_iota(jnp.int32, (2 * blk, blk), 0)
    qi = lax.broadcasted_iota(jnp.int32, (2 * blk, blk), 1)
    sink = sink_ref[...] * LOG2E
    ones = jnp.ones((VT_ROWS - hd, 2 * blk), BF16)
    for nb in range(tq // blk):
        start = base + nb * blk
        kstart = pl.multiple_of(jnp.maximum(start - blk, 0), blk)
        diff = qi + (start - kstart) - kk
        bias1 = jnp.where((diff >= 0) & (diff < WINDOW), 0.0, -jnp.inf)
        bias = jnp.concatenate([bias1] * group, axis=1)
        kb = k_ref[pl.ds(kstart, 2 * blk), :]
        vb = v_ref[pl.ds(kstart, 2 * blk), :].astype(F32)
        vt = jnp.concatenate([vb.T.astype(BF16), ones], axis=0)
        qt = q_ref[nb * blk:(nb + 1) * blk, :]
        qs = jnp.concatenate([qt[:, g * hd:(g + 1) * hd] for g in range(group)], axis=0)
        s_t = lax.dot_general(kb, qs, (((1,), (1,)), ((), ())), preferred_element_type=F32) + bias
        m = jnp.maximum(jnp.max(s_t, axis=0, keepdims=True), sink)
        p_t = jnp.exp2(s_t - m).astype(BF16)
        acc = jnp.dot(vt, p_t, preferred_element_type=F32)
        denom = acc[hd:hd + 1, :] + jnp.exp2(sink - m)
        o_t = acc[0:hd, :] * (1.0 / denom)
        for g in range(group):
            sl = slice(g * hd, (g + 1) * hd)
            z = z_ref[nb * blk:(nb + 1) * blk, sl].astype(F32)
            o_ref[nb * blk:(nb + 1) * blk, sl] = (
                o_t[:, sl].T * (z * jax.nn.sigmoid(z))).astype(o_ref.dtype)


def _swa_attention(qz, kv, sinks):
    B, S, W2 = qz.shape
    width = W2 // 2
    hq = width // HEAD_DIM
    group = hq // KV_HEADS
    gw = group * HEAD_DIM
    tq = _tile(S, 512)
    sink_b = jnp.broadcast_to(sinks.astype(F32).reshape(KV_HEADS, 1, group, 1),
                              (KV_HEADS, 1, group, LANES)).reshape(KV_HEADS, 1, group * LANES)
    return pl.pallas_call(
        functools.partial(_swa_kernel, group=group),
        grid=(B, KV_HEADS, S // tq),
        in_specs=[
            pl.BlockSpec((None, tq, gw), lambda b, g, i: (b, i, g)),
            pl.BlockSpec((None, tq, gw), lambda b, g, i: (b, i, KV_HEADS + g)),
            pl.BlockSpec((None, S, HEAD_DIM), lambda b, g, i: (b, 0, g)),
            pl.BlockSpec((None, S, HEAD_DIM), lambda b, g, i: (b, 0, KV_HEADS + g)),
            pl.BlockSpec((None, 1, group * LANES), lambda b, g, i: (g, 0, 0)),
        ],
        out_specs=pl.BlockSpec((None, tq, gw), lambda b, g, i: (b, i, g)),
        out_shape=jax.ShapeDtypeStruct((B, S, width), BF16),
        compiler_params=_params(VMEM_LIMIT_MID, ("parallel", "parallel", "parallel")),
        name="swa_attention",
    )(qz, qz, kv, kv, sink_b)


def kernel(x, c, norm_g, ada_w, ada_b, a_w_in, a_b_f, a_w_out, kv_norm_g, kv_w,
           b_w_in, b_sinks, b_w_out, final_norm_g):
    B, S, D = x.shape
    n_a = a_w_in.shape[0]
    n_b = b_w_in.shape[0]
    depth = n_a + n_b
    sm_scale = HEAD_DIM ** -0.5 * LOG2E

    mod = _adaln(c, ada_w, ada_b)
    rope_tabs = _rope_tables(S)

    h = x
    kv = None
    for layer in range(depth):
        mod3 = mod[layer].reshape(B, 1, 3 * D)
        if layer < n_a:
            a_heads = a_b_f.shape[1]
            a_width = a_heads * HEAD_DIM
            hn = _norm_mod(h, norm_g[layer], mod3)
            w_in = a_w_in[layer]
            qkvz = _proj(hn.reshape(B * S, D), w_in.astype(BF16), n_cols=4 * a_width,
                         lead_cols=a_width, lead_scale=sm_scale)
            aq, ak = _forget_columns(hn, w_in[:, 4 * a_width:], a_b_f[layer])
            o = _fox_attention(qkvz.reshape(B, S, 4 * a_width), aq, ak, a_heads)
            h = _out_proj(o.reshape(B * S, a_width), a_w_out[layer].astype(BF16), h, mod3)
        else:
            j = layer - n_a
            if kv is None:
                hkv, hn = _norm_dual(h, kv_norm_g, norm_g[layer], mod3)
                kvw = kv_w.shape[1] // 2
                kv = _proj(hkv.reshape(B * S, D), kv_w.astype(BF16), lead_cols=kvw,
                           rope_tabs=rope_tabs, seq=S, tn_pref=kvw).reshape(B, S, 2 * kvw)
            else:
                hn = _norm_mod(h, norm_g[layer], mod3)
            b_width = b_w_in.shape[2] // 2
            qz = _proj(hn.reshape(B * S, D), b_w_in[j].astype(BF16), lead_cols=b_width,
                       lead_scale=sm_scale, rope_tabs=rope_tabs, seq=S)
            o = _swa_attention(qz.reshape(B, S, 2 * b_width), kv, b_sinks[j])
            h = _out_proj(o.reshape(B * S, b_width), b_w_out[j].astype(BF16), h, mod3)
    return _final_norm(h, final_norm_g)
```

```python
import functools
import math

import jax
import jax.numpy as jnp
from jax import lax
from jax.experimental import pallas as pl
from jax.experimental.pallas import tpu as pltpu

HEAD_DIM = 128
KV_HEADS = 4
WINDOW = 128
ROPE_THETA = 10000.0
EPS = 1e-6

V7X_VMEM_BYTES = 64 * 1024 * 1024
VMEM_LIMIT_BIG = 56 * 1024 * 1024
VMEM_LIMIT_MID = 40 * 1024 * 1024
LANES = 128
SUBLANES = 8
BF16_SUBLANES = 16
VT_ROWS = HEAD_DIM + BF16_SUBLANES
LOG2E = math.log2(math.e)

F32 = jnp.float32
BF16 = jnp.bfloat16


def _params(limit, sem):
    return pltpu.CompilerParams(dimension_semantics=sem, vmem_limit_bytes=limit)


def _tile(n, pref):
    t = min(n, pref)
    assert n % t == 0, (n, t)
    return t


def _adaln_kernel(c_ref, w_ref, b_ref, o_ref):
    c = c_ref[...]
    sc = (c * jax.nn.sigmoid(c)).astype(BF16)
    acc = jnp.dot(sc, w_ref[...].astype(BF16), preferred_element_type=F32)
    o_ref[...] = acc + b_ref[...]


def _adaln(c, ada_w, ada_b):
    L, D, N = ada_w.shape
    B = c.shape[0]
    bp = -(-B // SUBLANES) * SUBLANES
    c_pad = jnp.zeros((bp, D), F32).at[:B].set(c)
    tn = _tile(N, 512)
    out = pl.pallas_call(
        _adaln_kernel,
        grid=(L, N // tn),
        in_specs=[
            pl.BlockSpec((bp, D), lambda l, j: (0, 0)),
            pl.BlockSpec((None, D, tn), lambda l, j: (l, 0, j)),
            pl.BlockSpec((None, 1, tn), lambda l, j: (l, 0, j)),
        ],
        out_specs=pl.BlockSpec((None, bp, tn), lambda l, j: (l, 0, j)),
        out_shape=jax.ShapeDtypeStruct((L, bp, N), F32),
        compiler_params=_params(VMEM_LIMIT_MID, ("parallel", "parallel")),
        name="adaln",
    )(c_pad, ada_w, ada_b.reshape(L, 1, N))
    return out[:, :B]


def _rms(x):
    return x * lax.rsqrt(jnp.mean(x * x, axis=-1, keepdims=True) + EPS)


def _norm_mod_kernel(x_ref, g_ref, shift_ref, scale_ref, o_ref):
    y = _rms(x_ref[...]) * g_ref[...]
    o_ref[...] = (y * (1.0 + scale_ref[...]) + shift_ref[...]).astype(o_ref.dtype)


def _norm_mod(x, g, mod3):
    B, S, D = x.shape
    ts = _tile(S, 256)
    return pl.pallas_call(
        _norm_mod_kernel,
        grid=(B, S // ts),
        in_specs=[
            pl.BlockSpec((None, ts, D), lambda b, i: (b, i, 0)),
            pl.BlockSpec((1, D), lambda b, i: (0, 0)),
            pl.BlockSpec((None, 1, D), lambda b, i: (b, 0, 0)),
            pl.BlockSpec((None, 1, D), lambda b, i: (b, 0, 1)),
        ],
        out_specs=pl.BlockSpec((None, ts, D), lambda b, i: (b, i, 0)),
        out_shape=jax.ShapeDtypeStruct((B, S, D), BF16),
        compiler_params=_params(VMEM_LIMIT_MID, ("parallel", "parallel")),
        name="norm_mod",
    )(x, g.reshape(1, D), mod3, mod3)


def _norm_dual_kernel(x_ref, gkv_ref, g_ref, shift_ref, scale_ref, okv_ref, o_ref):
    y = _rms(x_ref[...])
    okv_ref[...] = (y * gkv_ref[...]).astype(okv_ref.dtype)
    o_ref[...] = ((y * g_ref[...]) * (1.0 + scale_ref[...]) + shift_ref[...]).astype(o_ref.dtype)


def _norm_dual(x, g_kv, g, mod3):
    B, S, D = x.shape
    ts = _tile(S, 256)
    row = pl.BlockSpec((None, ts, D), lambda b, i: (b, i, 0))
    vec = pl.BlockSpec((1, D), lambda b, i: (0, 0))
    return pl.pallas_call(
        _norm_dual_kernel,
        grid=(B, S // ts),
        in_specs=[row, vec, vec,
                  pl.BlockSpec((None, 1, D), lambda b, i: (b, 0, 0)),
                  pl.BlockSpec((None, 1, D), lambda b, i: (b, 0, 1))],
        out_specs=[row, row],
        out_shape=[jax.ShapeDtypeStruct((B, S, D), BF16)] * 2,
        compiler_params=_params(VMEM_LIMIT_MID, ("parallel", "parallel")),
        name="norm_dual",
    )(x, g_kv.reshape(1, D), g.reshape(1, D), mod3, mod3)


def _final_norm_kernel(x_ref, g_ref, o_ref):
    o_ref[...] = _rms(x_ref[...]) * g_ref[...]


def _final_norm(x, g):
    B, S, D = x.shape
    ts = _tile(S, 256)
    row = pl.BlockSpec((None, ts, D), lambda b, i: (b, i, 0))
    return pl.pallas_call(
        _final_norm_kernel,
        grid=(B, S // ts),
        in_specs=[row, pl.BlockSpec((1, D), lambda b, i: (0, 0))],
        out_specs=row,
        out_shape=jax.ShapeDtypeStruct((B, S, D), F32),
        compiler_params=_params(VMEM_LIMIT_MID, ("parallel", "parallel")),
        name="final_norm",
    )(x, g.reshape(1, D))


def _rope_table_kernel(inv_ref, cos_ref, sin_ref):
    ts = cos_ref.shape[0]
    pos = (pl.program_id(0) * ts + lax.broadcasted_iota(jnp.int32, (ts, HEAD_DIM), 0)).astype(F32)
    ang = pos * inv_ref[...]
    lane = lax.broadcasted_iota(jnp.int32, (ts, HEAD_DIM), 1)
    cos_ref[...] = jnp.cos(ang)
    sin_ref[...] = jnp.where(lane < HEAD_DIM // 2, -jnp.sin(ang), jnp.sin(ang))


def _rope_tables(S):
    half = HEAD_DIM // 2
    inv = ROPE_THETA ** (-jnp.arange(half, dtype=F32) / half)
    inv2 = jnp.concatenate([inv, inv]).reshape(1, HEAD_DIM)
    ts = _tile(S, 512)
    tab = pl.BlockSpec((ts, HEAD_DIM), lambda i: (i, 0))
    return pl.pallas_call(
        _rope_table_kernel,
        grid=(S // ts,),
        in_specs=[pl.BlockSpec((1, HEAD_DIM), lambda i: (0, 0))],
        out_specs=[tab, tab],
        out_shape=[jax.ShapeDtypeStruct((S, HEAD_DIM), F32)] * 2,
        name="rope_tables",
    )(inv2)


def _proj_kernel(*refs, n_lead, lead_scale, rope):
    if rope:
        x_ref, w_ref, cos_ref, sin_ref, o_ref = refs
    else:
        x_ref, w_ref, o_ref = refs
    acc = jnp.dot(x_ref[...], w_ref[...], preferred_element_type=F32)
    tn = acc.shape[1]
    j = pl.program_id(1)

    def lead():
        if rope:
            cos, sin = cos_ref[...], sin_ref[...]
            for h in range(tn // HEAD_DIM):
                sl = slice(h * HEAD_DIM, (h + 1) * HEAD_DIM)
                xh = acc[:, sl]
                r = xh * cos + pltpu.roll(xh, HEAD_DIM // 2, axis=1) * sin
                o_ref[:, sl] = (r * lead_scale).astype(o_ref.dtype)
        else:
            o_ref[...] = (acc * lead_scale).astype(o_ref.dtype)

    def rest():
        o_ref[...] = acc.astype(o_ref.dtype)

    if n_lead == 0:
        rest()
    else:
        pl.when(j < n_lead)(lead)
        pl.when(j >= n_lead)(rest)


def _proj(x, w, *, n_cols=None, lead_cols=0, lead_scale=1.0, rope_tabs=None, seq=None,
          tn_pref=1024):
    M, K = x.shape
    N = w.shape[1] if n_cols is None else n_cols
    tm = _tile(M if seq is None else seq, 1024)
    tn = _tile(N, tn_pref)
    assert lead_cols % tn == 0 and tn % HEAD_DIM == 0
    in_specs = [pl.BlockSpec((tm, K), lambda i, j: (i, 0)),
                pl.BlockSpec((K, tn), lambda i, j: (0, j))]
    args = [x, w]
    if rope_tabs is not None:
        spt = seq // tm
        tab = pl.BlockSpec((tm, HEAD_DIM), lambda i, j: (i % spt, 0))
        in_specs += [tab, tab]
        args += list(rope_tabs)
    return pl.pallas_call(
        functools.partial(_proj_kernel, n_lead=lead_cols // tn, lead_scale=lead_scale,
                          rope=rope_tabs is not None),
        grid=(M // tm, N // tn),
        in_specs=in_specs,
        out_specs=pl.BlockSpec((tm, tn), lambda i, j: (i, j)),
        out_shape=jax.ShapeDtypeStruct((M, N), BF16),
        compiler_params=_params(VMEM_LIMIT_BIG, ("parallel", "parallel")),
        name="proj",
    )(*args)


def _cast_kernel(w_ref, o_ref):
    o_ref[...] = w_ref[...].astype(o_ref.dtype)


def _cast_cols(w, n_cols):
    K = w.shape[0]
    tk = _tile(K, 512)
    tn = _tile(n_cols, 2048)
    return pl.pallas_call(
        _cast_kernel,
        grid=(K // tk, n_cols // tn),
        in_specs=[pl.BlockSpec((tk, tn), lambda i, j: (i, j))],
        out_specs=pl.BlockSpec((tk, tn), lambda i, j: (i, j)),
        out_shape=jax.ShapeDtypeStruct((K, n_cols), BF16),
        compiler_params=_params(VMEM_LIMIT_MID, ("parallel", "parallel")),
        name="cast_cols",
    )(w)


def _out_proj_kernel(a_ref, w_ref, x_ref, gate_ref, o_ref):
    y = jnp.dot(a_ref[...], w_ref[...], preferred_element_type=F32)
    o_ref[...] = x_ref[...] + gate_ref[...] * y


def _out_proj(a, w, x, mod3):
    B, S, N = x.shape
    K = a.shape[1]
    tm = _tile(S, 1024)
    tn = _tile(N, 512)
    spt = S // tm
    nj = N // tn
    out = pl.pallas_call(
        _out_proj_kernel,
        grid=(B * spt, nj),
        in_specs=[
            pl.BlockSpec((tm, K), lambda i, j: (i, 0)),
            pl.BlockSpec((K, tn), lambda i, j: (0, j)),
            pl.BlockSpec((tm, tn), lambda i, j: (i, j)),
            pl.BlockSpec((None, 1, tn), lambda i, j: (i // spt, 0, 2 * nj + j)),
        ],
        out_specs=pl.BlockSpec((tm, tn), lambda i, j: (i, j)),
        out_shape=jax.ShapeDtypeStruct((B * S, N), F32),
        compiler_params=_params(VMEM_LIMIT_BIG, ("parallel", "parallel")),
        name="out_proj",
    )(a, w, x.reshape(B * S, N), mod3)
    return out.reshape(B, S, N)


def _split3(x):
    hi = x.astype(BF16)
    r1 = x - hi.astype(F32)
    mid = r1.astype(BF16)
    lo = (r1 - mid.astype(F32)).astype(BF16)
    return hi, mid, lo


def _forget_kernel(h_ref, w_ref, b_ref, aq_ref, ak_ref, carry_ref, *, n_heads):
    ts = h_ref.shape[0]

    @pl.when(pl.program_id(1) == 0)
    def _():
        carry_ref[...] = jnp.zeros_like(carry_ref)

    f = jnp.dot(h_ref[...], w_ref[...], preferred_element_type=F32) + b_ref[...]
    log_f = jnp.minimum(f, 0.0) - jnp.log1p(jnp.exp(-jnp.abs(f)))
    row = lax.broadcasted_iota(jnp.int32, (ts, ts), 0)
    col = lax.broadcasted_iota(jnp.int32, (ts, ts), 1)
    tri = jnp.where(row >= col, 1.0, 0.0).astype(BF16)
    cum = carry_ref[...]
    for part in _split3(log_f):
        cum = cum + jnp.dot(tri, part, preferred_element_type=F32)
    carry_ref[...] = cum[ts - 1:ts, :]

    hi, mid, lo = (p.astype(F32) for p in _split3(cum * LOG2E))
    lane = lax.broadcasted_iota(jnp.int32, (ts, LANES), 1)
    ones_q = jnp.where((lane >= 3) & (lane < 6), 1.0, 0.0)
    ones_k = jnp.where(lane < 3, 1.0, 0.0)
    for h in range(n_heads):
        c = slice(h, h + 1)
        fq = jnp.where(lane == 0, hi[:, c], jnp.where(lane == 1, mid[:, c],
                       jnp.where(lane == 2, lo[:, c], ones_q)))
        fk = jnp.where(lane == 3, -hi[:, c], jnp.where(lane == 4, -mid[:, c],
                       jnp.where(lane == 5, -lo[:, c], ones_k)))
        aq_ref[h] = fq.astype(BF16)
        ak_ref[h] = fk.astype(BF16)


def _forget_columns(h, w_f, b_f):
    B, S, D = h.shape
    H = w_f.shape[1]
    assert H <= LANES
    w_pad = jnp.zeros((D, LANES), BF16).at[:, :H].set(w_f.astype(BF16))
    b_pad = jnp.zeros((1, LANES), F32).at[0, :H].set(b_f.astype(F32))
    ts = _tile(S, 256)
    out = pl.BlockSpec((None, H, ts, LANES), lambda b, i: (b, 0, i, 0))
    return pl.pallas_call(
        functools.partial(_forget_kernel, n_heads=H),
        grid=(B, S // ts),
        in_specs=[
            pl.BlockSpec((None, ts, D), lambda b, i: (b, i, 0)),
            pl.BlockSpec((D, LANES), lambda b, i: (0, 0)),
            pl.BlockSpec((1, LANES), lambda b, i: (0, 0)),
        ],
        out_specs=[out, out],
        out_shape=[jax.ShapeDtypeStruct((B, H, S, LANES), BF16)] * 2,
        scratch_shapes=[pltpu.VMEM((1, LANES), F32)],
        compiler_params=_params(VMEM_LIMIT_MID, ("parallel", "arbitrary")),
        name="forget_columns",
    )(h, w_pad, b_pad)


def _fox_kernel(q_ref, k_ref, v_ref, z_ref, aq_ref, ak_ref, o_ref, vt_ref, *, tq, hp):
    S = q_ref.shape[0]
    nq = S // tq
    hd = HEAD_DIM
    ones = jnp.ones((VT_ROWS - hd, tq), BF16)
    for g in range(hp):
        for j in range(nq):
            vblk = v_ref[j * tq:(j + 1) * tq, g * hd:(g + 1) * hd].astype(F32)
            vt_ref[g, j, 0:hd, :] = vblk.T.astype(BF16)
            vt_ref[g, j, hd:VT_ROWS, :] = ones

    key = lax.broadcasted_iota(jnp.int32, (tq, tq), 0)
    qry = lax.broadcasted_iota(jnp.int32, (tq, tq), 1)
    causal_t = key <= qry

    def scores(g, qa, ks, masked):
        ka = jnp.concatenate([k_ref[pl.ds(ks, tq), g * hd:(g + 1) * hd],
                              ak_ref[g, pl.ds(ks, tq), :]], axis=1)
        s_t = lax.dot_general(ka, qa, (((1,), (1,)), ((), ())), preferred_element_type=F32)
        return jnp.where(causal_t, s_t, -jnp.inf) if masked else s_t

    def update(g, j, s_t, carry):
        m, acc = carry
        m_new = jnp.maximum(m, jnp.max(s_t, axis=0, keepdims=True))
        alpha = jnp.exp2(m - m_new)
        p_t = jnp.exp2(s_t - m_new).astype(BF16)
        acc = alpha * acc + jnp.dot(vt_ref[g, j], p_t, preferred_element_type=F32)
        return m_new, acc

    def steps(qas, j, ks, carry, masked):
        s_ts = [scores(g, qas[g], ks, masked) for g in range(hp)]
        return tuple(update(g, j, s_ts[g], carry[g]) for g in range(hp))

    for i in range(nq):
        qs = i * tq
        qas = [jnp.concatenate([q_ref[qs:qs + tq, g * hd:(g + 1) * hd], aq_ref[g, qs:qs + tq, :]],
                               axis=1) for g in range(hp)]
        carry = tuple((jnp.full((1, tq), -jnp.inf, F32), jnp.zeros((VT_ROWS, tq), F32))
                      for _ in range(hp))
        s_cur = [scores(g, qas[g], 0, i == 0) for g in range(hp)]
        for j in range(i + 1):
            if j < i:
                s_nxt = [scores(g, qas[g], (j + 1) * tq, j + 1 == i) for g in range(hp)]
            carry = tuple(update(g, j, s_cur[g], carry[g]) for g in range(hp))
            if j < i:
                s_cur = s_nxt
        for g in range(hp):
            acc = carry[g][1]
            o = (acc[0:hd, :] * (1.0 / acc[hd:hd + 1, :])).T
            z = z_ref[qs:qs + tq, g * hd:(g + 1) * hd].astype(F32)
            o_ref[qs:qs + tq, g * hd:(g + 1) * hd] = (
                o * (z * jax.nn.sigmoid(z))).astype(o_ref.dtype)


def _fox_attention(qkvz, aq, ak, n_heads):
    B, S, _ = qkvz.shape
    H = n_heads
    hp = 2 if H % 2 == 0 else 1
    tq = _tile(S, 512)
    w = hp * HEAD_DIM
    col = lambda c: pl.BlockSpec((None, S, w), lambda b, h: (b, 0, c * (H // hp) + h))
    side = pl.BlockSpec((None, hp, S, LANES), lambda b, h: (b, h, 0, 0))
    return pl.pallas_call(
        functools.partial(_fox_kernel, tq=tq, hp=hp),
        grid=(B, H // hp),
        in_specs=[col(0), col(1), col(2), col(3), side, side],
        out_specs=pl.BlockSpec((None, S, w), lambda b, h: (b, 0, h)),
        out_shape=jax.ShapeDtypeStruct((B, S, H * HEAD_DIM), BF16),
        scratch_shapes=[pltpu.VMEM((hp, S // tq, VT_ROWS, tq), BF16)],
        compiler_params=_params(VMEM_LIMIT_BIG, ("parallel", "parallel")),
        name="fox_attention",
    )(qkvz, qkvz, qkvz, qkvz, aq, ak)


def _swa_kernel(q_ref, z_ref, k_ref, v_ref, sink_ref, o_ref, *, group):
    tq = q_ref.shape[0]
    blk = WINDOW
    hd = HEAD_DIM
    base = pl.program_id(2) * tq
    kk = lax.broadcasted_iota(jnp.int32, (2 * blk, blk), 0)
    qi = lax.broadcasted_iota(jnp.int32, (2 * blk, blk), 1)
    sink = sink_ref[...] * LOG2E
    ones = jnp.ones((VT_ROWS - hd, 2 * blk), BF16)
    for nb in range(tq // blk):
        start = base + nb * blk
        kstart = pl.multiple_of(jnp.maximum(start - blk, 0), blk)
        diff = qi + (start - kstart) - kk
        bias1 = jnp.where((diff >= 0) & (diff < WINDOW), 0.0, -jnp.inf)
        bias = jnp.concatenate([bias1] * group, axis=1)
        kb = k_ref[pl.ds(kstart, 2 * blk), :]
        vb = v_ref[pl.ds(kstart, 2 * blk), :].astype(F32)
        vt = jnp.concatenate([vb.T.astype(BF16), ones], axis=0)
        qt = q_ref[nb * blk:(nb + 1) * blk, :]
        qs = jnp.concatenate([qt[:, g * hd:(g + 1) * hd] for g in range(group)], axis=0)
        s_t = lax.dot_general(kb, qs, (((1,), (1,)), ((), ())), preferred_element_type=F32) + bias
        m = jnp.maximum(jnp.max(s_t, axis=0, keepdims=True), sink)
        p_t = jnp.exp2(s_t - m).astype(BF16)
        acc = jnp.dot(vt, p_t, preferred_element_type=F32)
        denom = acc[hd:hd + 1, :] + jnp.exp2(sink - m)
        o_t = acc[0:hd, :] * (1.0 / denom)
        for g in range(group):
            sl = slice(g * hd, (g + 1) * hd)
            z = z_ref[nb * blk:(nb + 1) * blk, sl].astype(F32)
            o_ref[nb * blk:(nb + 1) * blk, sl] = (
                o_t[:, sl].T * (z * jax.nn.sigmoid(z))).astype(o_ref.dtype)


def _swa_attention(qz, kv, sinks):
    B, S, W2 = qz.shape
    width = W2 // 2
    hq = width // HEAD_DIM
    group = hq // KV_HEADS
    gw = group * HEAD_DIM
    tq = _tile(S, 512)
    sink_b = jnp.broadcast_to(sinks.astype(F32).reshape(KV_HEADS, 1, group, 1),
                              (KV_HEADS, 1, group, LANES)).reshape(KV_HEADS, 1, group * LANES)
    return pl.pallas_call(
        functools.partial(_swa_kernel, group=group),
        grid=(B, KV_HEADS, S // tq),
        in_specs=[
            pl.BlockSpec((None, tq, gw), lambda b, g, i: (b, i, g)),
            pl.BlockSpec((None, tq, gw), lambda b, g, i: (b, i, KV_HEADS + g)),
            pl.BlockSpec((None, S, HEAD_DIM), lambda b, g, i: (b, 0, g)),
            pl.BlockSpec((None, S, HEAD_DIM), lambda b, g, i: (b, 0, KV_HEADS + g)),
            pl.BlockSpec((None, 1, group * LANES), lambda b, g, i: (g, 0, 0)),
        ],
        out_specs=pl.BlockSpec((None, tq, gw), lambda b, g, i: (b, i, g)),
        out_shape=jax.ShapeDtypeStruct((B, S, width), BF16),
        compiler_params=_params(VMEM_LIMIT_MID, ("parallel", "parallel", "parallel")),
        name="swa_attention",
    )(qz, qz, kv, kv, sink_b)


def kernel(x, c, norm_g, ada_w, ada_b, a_w_in, a_b_f, a_w_out, kv_norm_g, kv_w,
           b_w_in, b_sinks, b_w_out, final_norm_g):
    B, S, D = x.shape
    n_a = a_w_in.shape[0]
    n_b = b_w_in.shape[0]
    depth = n_a + n_b
    sm_scale = HEAD_DIM ** -0.5 * LOG2E

    mod = _adaln(c, ada_w, ada_b)
    rope_tabs = _rope_tables(S)

    h = x
    kv = None
    for layer in range(depth):
        mod3 = mod[layer].reshape(B, 1, 3 * D)
        if layer < n_a:
            a_heads = a_b_f.shape[1]
            a_width = a_heads * HEAD_DIM
            hn = _norm_mod(h, norm_g[layer], mod3)
            w_in = a_w_in[layer]
            qkvz = _proj(hn.reshape(B * S, D), _cast_cols(w_in, 4 * a_width),
                         lead_cols=a_width, lead_scale=sm_scale)
            aq, ak = _forget_columns(hn, w_in[:, 4 * a_width:], a_b_f[layer])
            o = _fox_attention(qkvz.reshape(B, S, 4 * a_width), aq, ak, a_heads)
            h = _out_proj(o.reshape(B * S, a_width), a_w_out[layer].astype(BF16), h, mod3)
        else:
            j = layer - n_a
            if kv is None:
                hkv, hn = _norm_dual(h, kv_norm_g, norm_g[layer], mod3)
                kvw = kv_w.shape[1] // 2
                kv = _proj(hkv.reshape(B * S, D), kv_w.astype(BF16), lead_cols=kvw,
                           rope_tabs=rope_tabs, seq=S, tn_pref=kvw).reshape(B, S, 2 * kvw)
            else:
                hn = _norm_mod(h, norm_g[layer], mod3)
            b_width = b_w_in.shape[2] // 2
            qz = _proj(hn.reshape(B * S, D), b_w_in[j].astype(BF16), lead_cols=b_width,
                       lead_scale=sm_scale, rope_tabs=rope_tabs, seq=S)
            o = _swa_attention(qz.reshape(B, S, 2 * b_width), kv, b_sinks[j])
            h = _out_proj(o.reshape(B * S, b_width), b_w_out[j].astype(BF16), h, mod3)
    return _final_norm(h, final_norm_g)
```

```python
import functools
import math

import jax
import jax.numpy as jnp
from jax import lax
from jax.experimental import pallas as pl
from jax.experimental.pallas import tpu as pltpu

HEAD_DIM = 128
KV_HEADS = 4
WINDOW = 128
ROPE_THETA = 10000.0
EPS = 1e-6

V7X_VMEM_BYTES = 64 * 1024 * 1024
VMEM_LIMIT_BIG = 56 * 1024 * 1024
VMEM_LIMIT_MID = 40 * 1024 * 1024
LANES = 128
SUBLANES = 8
BF16_SUBLANES = 16
VT_ROWS = HEAD_DIM + BF16_SUBLANES
LOG2E = math.log2(math.e)
NORM_ROWS = 512

F32 = jnp.float32
BF16 = jnp.bfloat16


def _params(limit, sem):
    return pltpu.CompilerParams(dimension_semantics=sem, vmem_limit_bytes=limit)


def _tile(n, pref):
    t = min(n, pref)
    assert n % t == 0, (n, t)
    return t


def _adaln_kernel(c_ref, w_ref, b_ref, o_ref):
    c = c_ref[...]
    sc = (c * jax.nn.sigmoid(c)).astype(BF16)
    acc = jnp.dot(sc, w_ref[...].astype(BF16), preferred_element_type=F32)
    o_ref[...] = acc + b_ref[...]


def _adaln(c, ada_w, ada_b):
    L, D, N = ada_w.shape
    B = c.shape[0]
    bp = -(-B // SUBLANES) * SUBLANES
    c_pad = jnp.zeros((bp, D), F32).at[:B].set(c)
    tn = _tile(N, 512)
    out = pl.pallas_call(
        _adaln_kernel,
        grid=(L, N // tn),
        in_specs=[
            pl.BlockSpec((bp, D), lambda l, j: (0, 0)),
            pl.BlockSpec((None, D, tn), lambda l, j: (l, 0, j)),
            pl.BlockSpec((None, 1, tn), lambda l, j: (l, 0, j)),
        ],
        out_specs=pl.BlockSpec((None, bp, tn), lambda l, j: (l, 0, j)),
        out_shape=jax.ShapeDtypeStruct((L, bp, N), F32),
        compiler_params=_params(VMEM_LIMIT_MID, ("parallel", "parallel")),
        name="adaln",
    )(c_pad, ada_w, ada_b.reshape(L, 1, N))
    return out[:, :B]


def _rms(x):
    return x * lax.rsqrt(jnp.mean(x * x, axis=-1, keepdims=True) + EPS)


def _norm_mod_kernel(x_ref, g_ref, shift_ref, scale_ref, o_ref):
    y = _rms(x_ref[...]) * g_ref[...]
    o_ref[...] = (y * (1.0 + scale_ref[...]) + shift_ref[...]).astype(o_ref.dtype)


def _norm_mod(x, g, mod3):
    B, S, D = x.shape
    ts = _tile(S, NORM_ROWS)
    return pl.pallas_call(
        _norm_mod_kernel,
        grid=(B, S // ts),
        in_specs=[
            pl.BlockSpec((None, ts, D), lambda b, i: (b, i, 0)),
            pl.BlockSpec((1, D), lambda b, i: (0, 0)),
            pl.BlockSpec((None, 1, D), lambda b, i: (b, 0, 0)),
            pl.BlockSpec((None, 1, D), lambda b, i: (b, 0, 1)),
        ],
        out_specs=pl.BlockSpec((None, ts, D), lambda b, i: (b, i, 0)),
        out_shape=jax.ShapeDtypeStruct((B, S, D), BF16),
        compiler_params=_params(VMEM_LIMIT_MID, ("parallel", "parallel")),
        name="norm_mod",
    )(x, g.reshape(1, D), mod3, mod3)


def _norm_dual_kernel(x_ref, gkv_ref, g_ref, shift_ref, scale_ref, okv_ref, o_ref):
    y = _rms(x_ref[...])
    okv_ref[...] = (y * gkv_ref[...]).astype(okv_ref.dtype)
    o_ref[...] = ((y * g_ref[...]) * (1.0 + scale_ref[...]) + shift_ref[...]).astype(o_ref.dtype)


def _norm_dual(x, g_kv, g, mod3):
    B, S, D = x.shape
    ts = _tile(S, NORM_ROWS)
    row = pl.BlockSpec((None, ts, D), lambda b, i: (b, i, 0))
    vec = pl.BlockSpec((1, D), lambda b, i: (0, 0))
    return pl.pallas_call(
        _norm_dual_kernel,
        grid=(B, S // ts),
        in_specs=[row, vec, vec,
                  pl.BlockSpec((None, 1, D), lambda b, i: (b, 0, 0)),
                  pl.BlockSpec((None, 1, D), lambda b, i: (b, 0, 1))],
        out_specs=[row, row],
        out_shape=[jax.ShapeDtypeStruct((B, S, D), BF16)] * 2,
        compiler_params=_params(VMEM_LIMIT_BIG, ("parallel", "parallel")),
        name="norm_dual",
    )(x, g_kv.reshape(1, D), g.reshape(1, D), mod3, mod3)


def _final_norm_kernel(x_ref, g_ref, o_ref):
    o_ref[...] = _rms(x_ref[...]) * g_ref[...]


def _final_norm(x, g):
    B, S, D = x.shape
    ts = _tile(S, NORM_ROWS)
    row = pl.BlockSpec((None, ts, D), lambda b, i: (b, i, 0))
    return pl.pallas_call(
        _final_norm_kernel,
        grid=(B, S // ts),
        in_specs=[row, pl.BlockSpec((1, D), lambda b, i: (0, 0))],
        out_specs=row,
        out_shape=jax.ShapeDtypeStruct((B, S, D), F32),
        compiler_params=_params(VMEM_LIMIT_BIG, ("parallel", "parallel")),
        name="final_norm",
    )(x, g.reshape(1, D))


def _rope_table_kernel(inv_ref, cos_ref, sin_ref):
    ts = cos_ref.shape[0]
    pos = (pl.program_id(0) * ts + lax.broadcasted_iota(jnp.int32, (ts, HEAD_DIM), 0)).astype(F32)
    ang = pos * inv_ref[...]
    lane = lax.broadcasted_iota(jnp.int32, (ts, HEAD_DIM), 1)
    cos_ref[...] = jnp.cos(ang)
    sin_ref[...] = jnp.where(lane < HEAD_DIM // 2, -jnp.sin(ang), jnp.sin(ang))


def _rope_tables(S):
    half = HEAD_DIM // 2
    inv = ROPE_THETA ** (-jnp.arange(half, dtype=F32) / half)
    inv2 = jnp.concatenate([inv, inv]).reshape(1, HEAD_DIM)
    ts = _tile(S, 512)
    tab = pl.BlockSpec((ts, HEAD_DIM), lambda i: (i, 0))
    return pl.pallas_call(
        _rope_table_kernel,
        grid=(S // ts,),
        in_specs=[pl.BlockSpec((1, HEAD_DIM), lambda i: (0, 0))],
        out_specs=[tab, tab],
        out_shape=[jax.ShapeDtypeStruct((S, HEAD_DIM), F32)] * 2,
        name="rope_tables",
    )(inv2)


def _proj_kernel(*refs, n_lead, lead_scale, rope, w_rows):
    if rope:
        x_ref, w_ref, cos_ref, sin_ref, o_ref = refs
    else:
        x_ref, w_ref, o_ref = refs
    contract = (((1,), (1 if w_rows else 0,)), ((), ()))
    acc = lax.dot_general(x_ref[...], w_ref[...], contract, preferred_element_type=F32)
    tn = acc.shape[1]
    j = pl.program_id(1)

    def lead():
        if rope:
            cos, sin = cos_ref[...], sin_ref[...]
            for h in range(tn // HEAD_DIM):
                sl = slice(h * HEAD_DIM, (h + 1) * HEAD_DIM)
                xh = acc[:, sl]
                r = xh * cos + pltpu.roll(xh, HEAD_DIM // 2, axis=1) * sin
                o_ref[:, sl] = (r * lead_scale).astype(o_ref.dtype)
        else:
            o_ref[...] = (acc * lead_scale).astype(o_ref.dtype)

    def rest():
        o_ref[...] = acc.astype(o_ref.dtype)

    if n_lead == 0:
        rest()
    else:
        pl.when(j < n_lead)(lead)
        pl.when(j >= n_lead)(rest)


def _proj(x, w, *, w_rows=False, lead_cols=0, lead_scale=1.0, rope_tabs=None, seq=None,
          tn_pref=1024):
    M, K = x.shape
    N = w.shape[0] if w_rows else w.shape[1]
    tm = _tile(M if seq is None else seq, 1024)
    tn = _tile(N, tn_pref)
    assert lead_cols % tn == 0 and tn % HEAD_DIM == 0
    w_spec = (pl.BlockSpec((tn, K), lambda i, j: (j, 0)) if w_rows
              else pl.BlockSpec((K, tn), lambda i, j: (0, j)))
    in_specs = [pl.BlockSpec((tm, K), lambda i, j: (i, 0)), w_spec]
    args = [x, w]
    if rope_tabs is not None:
        spt = seq // tm
        tab = pl.BlockSpec((tm, HEAD_DIM), lambda i, j: (i % spt, 0))
        in_specs += [tab, tab]
        args += list(rope_tabs)
    return pl.pallas_call(
        functools.partial(_proj_kernel, n_lead=lead_cols // tn, lead_scale=lead_scale,
                          rope=rope_tabs is not None, w_rows=w_rows),
        grid=(M // tm, N // tn),
        in_specs=in_specs,
        out_specs=pl.BlockSpec((tm, tn), lambda i, j: (i, j)),
        out_shape=jax.ShapeDtypeStruct((M, N), BF16),
        compiler_params=_params(VMEM_LIMIT_BIG, ("parallel", "parallel")),
        name="proj",
    )(*args)


def _cast_kernel(w_ref, o_ref):
    o_ref[...] = w_ref[...].astype(o_ref.dtype)


def _cast_rows(w_t, layer, n_rows):
    K = w_t.shape[2]
    tr = _tile(n_rows, 512)
    return pl.pallas_call(
        _cast_kernel,
        grid=(n_rows // tr,),
        in_specs=[pl.BlockSpec((None, tr, K), lambda i: (layer, i, 0))],
        out_specs=pl.BlockSpec((tr, K), lambda i: (i, 0)),
        out_shape=jax.ShapeDtypeStruct((n_rows, K), BF16),
        compiler_params=_params(VMEM_LIMIT_MID, ("parallel",)),
        name="cast_rows",
    )(w_t)


def _out_proj_kernel(a_ref, w_ref, x_ref, gate_ref, o_ref):
    y = jnp.dot(a_ref[...], w_ref[...], preferred_element_type=F32)
    o_ref[...] = x_ref[...] + gate_ref[...] * y


def _out_proj(a, w, x, mod3):
    B, S, N = x.shape
    K = a.shape[1]
    tm = _tile(S, 1024)
    tn = _tile(N, 512)
    spt = S // tm
    nj = N // tn
    out = pl.pallas_call(
        _out_proj_kernel,
        grid=(B * spt, nj),
        in_specs=[
            pl.BlockSpec((tm, K), lambda i, j: (i, 0)),
            pl.BlockSpec((K, tn), lambda i, j: (0, j)),
            pl.BlockSpec((tm, tn), lambda i, j: (i, j)),
            pl.BlockSpec((None, 1, tn), lambda i, j: (i // spt, 0, 2 * nj + j)),
        ],
        out_specs=pl.BlockSpec((tm, tn), lambda i, j: (i, j)),
        out_shape=jax.ShapeDtypeStruct((B * S, N), F32),
        compiler_params=_params(VMEM_LIMIT_BIG, ("parallel", "parallel")),
        name="out_proj",
    )(a, w, x.reshape(B * S, N), mod3)
    return out.reshape(B, S, N)


def _split3(x):
    hi = x.astype(BF16)
    r1 = x - hi.astype(F32)
    mid = r1.astype(BF16)
    lo = (r1 - mid.astype(F32)).astype(BF16)
    return hi, mid, lo


def _forget_kernel(h_ref, w_ref, b_ref, aq_ref, ak_ref, carry_ref, *, n_heads):
    ts = h_ref.shape[0]

    @pl.when(pl.program_id(1) == 0)
    def _():
        carry_ref[...] = jnp.zeros_like(carry_ref)

    w_t = w_ref[...].astype(BF16)
    w_t = jnp.concatenate([w_t, jnp.zeros((LANES - w_t.shape[0], w_t.shape[1]), BF16)], axis=0)
    f = lax.dot_general(h_ref[...], w_t, (((1,), (1,)), ((), ())),
                        preferred_element_type=F32) + b_ref[...]
    log_f = jnp.minimum(f, 0.0) - jnp.log1p(jnp.exp(-jnp.abs(f)))
    row = lax.broadcasted_iota(jnp.int32, (ts, ts), 0)
    col = lax.broadcasted_iota(jnp.int32, (ts, ts), 1)
    tri = jnp.where(row >= col, 1.0, 0.0).astype(BF16)
    cum = carry_ref[...]
    for part in _split3(log_f):
        cum = cum + jnp.dot(tri, part, preferred_element_type=F32)
    carry_ref[...] = cum[ts - 1:ts, :]

    hi, mid, lo = (p.astype(F32) for p in _split3(cum * LOG2E))
    lane = lax.broadcasted_iota(jnp.int32, (ts, LANES), 1)
    ones_q = jnp.where((lane >= 3) & (lane < 6), 1.0, 0.0)
    ones_k = jnp.where(lane < 3, 1.0, 0.0)
    for h in range(n_heads):
        c = slice(h, h + 1)
        fq = jnp.where(lane == 0, hi[:, c], jnp.where(lane == 1, mid[:, c],
                       jnp.where(lane == 2, lo[:, c], ones_q)))
        fk = jnp.where(lane == 3, -hi[:, c], jnp.where(lane == 4, -mid[:, c],
                       jnp.where(lane == 5, -lo[:, c], ones_k)))
        aq_ref[h] = fq.astype(BF16)
        ak_ref[h] = fk.astype(BF16)


def _forget_columns(h, w_t, layer, row0, b_f):
    B, S, D = h.shape
    H = b_f.shape[0]
    assert H <= LANES and H % BF16_SUBLANES == 0 and row0 % H == 0
    b_pad = jnp.zeros((1, LANES), F32).at[0, :H].set(b_f.astype(F32))
    ts = _tile(S, 256)
    out = pl.BlockSpec((None, H, ts, LANES), lambda b, i: (b, 0, i, 0))
    return pl.pallas_call(
        functools.partial(_forget_kernel, n_heads=H),
        grid=(B, S // ts),
        in_specs=[
            pl.BlockSpec((None, ts, D), lambda b, i: (b, i, 0)),
            pl.BlockSpec((None, H, D), lambda b, i: (layer, row0 // H, 0)),
            pl.BlockSpec((1, LANES), lambda b, i: (0, 0)),
        ],
        out_specs=[out, out],
        out_shape=[jax.ShapeDtypeStruct((B, H, S, LANES), BF16)] * 2,
        scratch_shapes=[pltpu.VMEM((1, LANES), F32)],
        compiler_params=_params(VMEM_LIMIT_MID, ("parallel", "arbitrary")),
        name="forget_columns",
    )(h, w_t, b_pad)


def _fox_kernel(q_ref, k_ref, v_ref, z_ref, aq_ref, ak_ref, o_ref, vt_ref, *, tq, hp):
    S = q_ref.shape[0]
    nq = S // tq
    hd = HEAD_DIM
    ones = jnp.ones((VT_ROWS - hd, tq), BF16)
    for g in range(hp):
        for j in range(nq):
            vblk = v_ref[j * tq:(j + 1) * tq, g * hd:(g + 1) * hd].astype(F32)
            vt_ref[g, j, 0:hd, :] = vblk.T.astype(BF16)
            vt_ref[g, j, hd:VT_ROWS, :] = ones

    key = lax.broadcasted_iota(jnp.int32, (tq, tq), 0)
    qry = lax.broadcasted_iota(jnp.int32, (tq, tq), 1)
    causal_t = key <= qry

    def scores(g, qa, ks, masked):
        ka = jnp.concatenate([k_ref[pl.ds(ks, tq), g * hd:(g + 1) * hd],
                              ak_ref[g, pl.ds(ks, tq), :]], axis=1)
        s_t = lax.dot_general(ka, qa, (((1,), (1,)), ((), ())), preferred_element_type=F32)
        return jnp.where(causal_t, s_t, -jnp.inf) if masked else s_t

    def update(g, j, s_t, carry):
        m, acc = carry
        m_new = jnp.maximum(m, jnp.max(s_t, axis=0, keepdims=True))
        alpha = jnp.exp2(m - m_new)
        p_t = jnp.exp2(s_t - m_new).astype(BF16)
        acc = alpha * acc + jnp.dot(vt_ref[g, j], p_t, preferred_element_type=F32)
        return m_new, acc

    def steps(qas, j, ks, carry, masked):
        s_ts = [scores(g, qas[g], ks, masked) for g in range(hp)]
        return tuple(update(g, j, s_ts[g], carry[g]) for g in range(hp))

    for i in range(nq):
        qs = i * tq
        qas = [jnp.concatenate([q_ref[qs:qs + tq, g * hd:(g + 1) * hd], aq_ref[g, qs:qs + tq, :]],
                               axis=1) for g in range(hp)]
        carry = tuple((jnp.full((1, tq), -jnp.inf, F32), jnp.zeros((VT_ROWS, tq), F32))
                      for _ in range(hp))
        s_cur = [scores(g, qas[g], 0, i == 0) for g in range(hp)]
        for j in range(i + 1):
            if j < i:
                s_nxt = [scores(g, qas[g], (j + 1) * tq, j + 1 == i) for g in range(hp)]
            carry = tuple(update(g, j, s_cur[g], carry[g]) for g in range(hp))
            if j < i:
                s_cur = s_nxt
        for g in range(hp):
            acc = carry[g][1]
            o = (acc[0:hd, :] * (1.0 / acc[hd:hd + 1, :])).T
            z = z_ref[qs:qs + tq, g * hd:(g + 1) * hd].astype(F32)
            o_ref[qs:qs + tq, g * hd:(g + 1) * hd] = (
                o * (z * jax.nn.sigmoid(z))).astype(o_ref.dtype)


def _fox_attention(qkvz, aq, ak, n_heads):
    B, S, _ = qkvz.shape
    H = n_heads
    hp = 2 if H % 2 == 0 else 1
    tq = _tile(S, 512)
    w = hp * HEAD_DIM
    col = lambda c: pl.BlockSpec((None, S, w), lambda b, h: (b, 0, c * (H // hp) + h))
    side = pl.BlockSpec((None, hp, S, LANES), lambda b, h: (b, h, 0, 0))
    return pl.pallas_call(
        functools.partial(_fox_kernel, tq=tq, hp=hp),
        grid=(B, H // hp),
        in_specs=[col(0), col(1), col(2), col(3), side, side],
        out_specs=pl.BlockSpec((None, S, w), lambda b, h: (b, 0, h)),
        out_shape=jax.ShapeDtypeStruct((B, S, H * HEAD_DIM), BF16),
        scratch_shapes=[pltpu.VMEM((hp, S // tq, VT_ROWS, tq), BF16)],
        compiler_params=_params(VMEM_LIMIT_BIG, ("parallel", "parallel")),
        name="fox_attention",
    )(qkvz, qkvz, qkvz, qkvz, aq, ak)


def _swa_kernel(q_ref, z_ref, k_ref, v_ref, sink_ref, o_ref, *, group):
    tq = q_ref.shape[0]
    blk = WINDOW
    hd = HEAD_DIM
    base = pl.program_id(2) * tq
    kk = lax.broadcasted_iota(jnp.int32, (2 * blk, blk), 0)
    qi = lax.broadcasted_iota(jnp.int32, (2 * blk, blk), 1)
    sink = sink_ref[...] * LOG2E
    ones = jnp.ones((VT_ROWS - hd, 2 * blk), BF16)
    for nb in range(tq // blk):
        start = base + nb * blk
        kstart = pl.multiple_of(jnp.maximum(start - blk, 0), blk)
        diff = qi + (start - kstart) - kk
        bias1 = jnp.where((diff >= 0) & (diff < WINDOW), 0.0, -jnp.inf)
        bias = jnp.concatenate([bias1] * group, axis=1)
        kb = k_ref[pl.ds(kstart, 2 * blk), :]
        vb = v_ref[pl.ds(kstart, 2 * blk), :].astype(F32)
        vt = jnp.concatenate([vb.T.astype(BF16), ones], axis=0)
        qt = q_ref[nb * blk:(nb + 1) * blk, :]
        qs = jnp.concatenate([qt[:, g * hd:(g + 1) * hd] for g in range(group)], axis=0)
        s_t = lax.dot_general(kb, qs, (((1,), (1,)), ((), ())), preferred_element_type=F32) + bias
        m = jnp.maximum(jnp.max(s_t, axis=0, keepdims=True), sink)
        p_t = jnp.exp2(s_t - m).astype(BF16)
        acc = jnp.dot(vt, p_t, preferred_element_type=F32)
        denom = acc[hd:hd + 1, :] + jnp.exp2(sink - m)
        o_t = acc[0:hd, :] * (1.0 / denom)
        for g in range(group):
            sl = slice(g * hd, (g + 1) * hd)
            z = z_ref[nb * blk:(nb + 1) * blk, sl].astype(F32)
            o_ref[nb * blk:(nb + 1) * blk, sl] = (
                o_t[:, sl].T * (z * jax.nn.sigmoid(z))).astype(o_ref.dtype)


def _swa_attention(qz, kv, sinks):
    B, S, W2 = qz.shape
    width = W2 // 2
    hq = width // HEAD_DIM
    group = hq // KV_HEADS
    gw = group * HEAD_DIM
    tq = _tile(S, 512)
    sink_b = jnp.broadcast_to(sinks.astype(F32).reshape(KV_HEADS, 1, group, 1),
                              (KV_HEADS, 1, group, LANES)).reshape(KV_HEADS, 1, group * LANES)
    return pl.pallas_call(
        functools.partial(_swa_kernel, group=group),
        grid=(B, KV_HEADS, S // tq),
        in_specs=[
            pl.BlockSpec((None, tq, gw), lambda b, g, i: (b, i, g)),
            pl.BlockSpec((None, tq, gw), lambda b, g, i: (b, i, KV_HEADS + g)),
            pl.BlockSpec((None, S, HEAD_DIM), lambda b, g, i: (b, 0, g)),
            pl.BlockSpec((None, S, HEAD_DIM), lambda b, g, i: (b, 0, KV_HEADS + g)),
            pl.BlockSpec((None, 1, group * LANES), lambda b, g, i: (g, 0, 0)),
        ],
        out_specs=pl.BlockSpec((None, tq, gw), lambda b, g, i: (b, i, g)),
        out_shape=jax.ShapeDtypeStruct((B, S, width), BF16),
        compiler_params=_params(VMEM_LIMIT_MID, ("parallel", "parallel", "parallel")),
        name="swa_attention",
    )(qz, qz, kv, kv, sink_b)


def kernel(x, c, norm_g, ada_w, ada_b, a_w_in, a_b_f, a_w_out, kv_norm_g, kv_w,
           b_w_in, b_sinks, b_w_out, final_norm_g):
    B, S, D = x.shape
    n_a = a_w_in.shape[0]
    n_b = b_w_in.shape[0]
    depth = n_a + n_b
    sm_scale = HEAD_DIM ** -0.5 * LOG2E

    mod = _adaln(c, ada_w, ada_b)
    rope_tabs = _rope_tables(S)

    h = x
    kv = None
    for layer in range(depth):
        mod3 = mod[layer].reshape(B, 1, 3 * D)
        if layer < n_a:
            a_heads = a_b_f.shape[1]
            a_width = a_heads * HEAD_DIM
            hn = _norm_mod(h, norm_g[layer], mod3)
            w_in_t = jnp.swapaxes(a_w_in, 1, 2)
            qkvz = _proj(hn.reshape(B * S, D), _cast_rows(w_in_t, layer, 4 * a_width),
                         w_rows=True, lead_cols=a_width, lead_scale=sm_scale)
            aq, ak = _forget_columns(hn, w_in_t, layer, 4 * a_width, a_b_f[layer])
            o = _fox_attention(qkvz.reshape(B, S, 4 * a_width), aq, ak, a_heads)
            h = _out_proj(o.reshape(B * S, a_width), a_w_out[layer].astype(BF16), h, mod3)
        else:
            j = layer - n_a
            if kv is None:
                hkv, hn = _norm_dual(h, kv_norm_g, norm_g[layer], mod3)
                kvw = kv_w.shape[1] // 2
                kv = _proj(hkv.reshape(B * S, D), kv_w.astype(BF16), lead_cols=kvw,
                           rope_tabs=rope_tabs, seq=S, tn_pref=kvw).reshape(B, S, 2 * kvw)
            else:
                hn = _norm_mod(h, norm_g[layer], mod3)
            b_width = b_w_in.shape[2] // 2
            qz = _proj(hn.reshape(B * S, D), b_w_in[j].astype(BF16), lead_cols=b_width,
                       lead_scale=sm_scale, rope_tabs=rope_tabs, seq=S)
            o = _swa_attention(qz.reshape(B, S, 2 * b_width), kv, b_sinks[j])
            h = _out_proj(o.reshape(B * S, b_width), b_w_out[j].astype(BF16), h, mod3)
    return _final_norm(h, final_norm_g)
```

```python
import functools
import math

import jax
import jax.numpy as jnp
from jax import lax
from jax.experimental import pallas as pl
from jax.experimental.pallas import tpu as pltpu

HEAD_DIM = 128
KV_HEADS = 4
WINDOW = 128
ROPE_THETA = 10000.0
EPS = 1e-6

V7X_VMEM_BYTES = 64 * 1024 * 1024
VMEM_LIMIT_BIG = 56 * 1024 * 1024
VMEM_LIMIT_MID = 40 * 1024 * 1024
LANES = 128
SUBLANES = 8
BF16_SUBLANES = 16
VT_ROWS = HEAD_DIM + BF16_SUBLANES
LOG2E = math.log2(math.e)
NORM_ROWS = 512

F32 = jnp.float32
BF16 = jnp.bfloat16


def _params(limit, sem):
    return pltpu.CompilerParams(dimension_semantics=sem, vmem_limit_bytes=limit)


def _tile(n, pref):
    t = min(n, pref)
    assert n % t == 0, (n, t)
    return t


def _adaln_kernel(c_ref, w_ref, b_ref, o_ref):
    c = c_ref[...]
    sc = (c * jax.nn.sigmoid(c)).astype(BF16)
    acc = jnp.dot(sc, w_ref[...].astype(BF16), preferred_element_type=F32)
    o_ref[...] = acc + b_ref[...]


def _adaln(c, ada_w, ada_b):
    L, D, N = ada_w.shape
    B = c.shape[0]
    bp = -(-B // SUBLANES) * SUBLANES
    c_pad = jnp.zeros((bp, D), F32).at[:B].set(c)
    tn = _tile(N, 512)
    out = pl.pallas_call(
        _adaln_kernel,
        grid=(L, N // tn),
        in_specs=[
            pl.BlockSpec((bp, D), lambda l, j: (0, 0)),
            pl.BlockSpec((None, D, tn), lambda l, j: (l, 0, j)),
            pl.BlockSpec((None, 1, tn), lambda l, j: (l, 0, j)),
        ],
        out_specs=pl.BlockSpec((None, bp, tn), lambda l, j: (l, 0, j)),
        out_shape=jax.ShapeDtypeStruct((L, bp, N), F32),
        compiler_params=_params(VMEM_LIMIT_MID, ("parallel", "parallel")),
        name="adaln",
    )(c_pad, ada_w, ada_b.reshape(L, 1, N))
    return out[:, :B]


def _rms(x):
    return x * lax.rsqrt(jnp.mean(x * x, axis=-1, keepdims=True) + EPS)


def _norm_mod_kernel(x_ref, g_ref, shift_ref, scale_ref, o_ref):
    y = _rms(x_ref[...]) * g_ref[...]
    o_ref[...] = (y * (1.0 + scale_ref[...]) + shift_ref[...]).astype(o_ref.dtype)


def _norm_mod(x, g, mod3):
    B, S, D = x.shape
    ts = _tile(S, NORM_ROWS)
    return pl.pallas_call(
        _norm_mod_kernel,
        grid=(B, S // ts),
        in_specs=[
            pl.BlockSpec((None, ts, D), lambda b, i: (b, i, 0)),
            pl.BlockSpec((1, D), lambda b, i: (0, 0)),
            pl.BlockSpec((None, 1, D), lambda b, i: (b, 0, 0)),
            pl.BlockSpec((None, 1, D), lambda b, i: (b, 0, 1)),
        ],
        out_specs=pl.BlockSpec((None, ts, D), lambda b, i: (b, i, 0)),
        out_shape=jax.ShapeDtypeStruct((B, S, D), BF16),
        compiler_params=_params(VMEM_LIMIT_MID, ("parallel", "parallel")),
        name="norm_mod",
    )(x, g.reshape(1, D), mod3, mod3)


def _norm_dual_kernel(x_ref, gkv_ref, g_ref, shift_ref, scale_ref, okv_ref, o_ref):
    y = _rms(x_ref[...])
    okv_ref[...] = (y * gkv_ref[...]).astype(okv_ref.dtype)
    o_ref[...] = ((y * g_ref[...]) * (1.0 + scale_ref[...]) + shift_ref[...]).astype(o_ref.dtype)


def _norm_dual(x, g_kv, g, mod3):
    B, S, D = x.shape
    ts = _tile(S, NORM_ROWS)
    row = pl.BlockSpec((None, ts, D), lambda b, i: (b, i, 0))
    vec = pl.BlockSpec((1, D), lambda b, i: (0, 0))
    return pl.pallas_call(
        _norm_dual_kernel,
        grid=(B, S // ts),
        in_specs=[row, vec, vec,
                  pl.BlockSpec((None, 1, D), lambda b, i: (b, 0, 0)),
                  pl.BlockSpec((None, 1, D), lambda b, i: (b, 0, 1))],
        out_specs=[row, row],
        out_shape=[jax.ShapeDtypeStruct((B, S, D), BF16)] * 2,
        compiler_params=_params(VMEM_LIMIT_BIG, ("parallel", "parallel")),
        name="norm_dual",
    )(x, g_kv.reshape(1, D), g.reshape(1, D), mod3, mod3)


def _final_norm_kernel(x_ref, g_ref, o_ref):
    o_ref[...] = _rms(x_ref[...]) * g_ref[...]


def _final_norm(x, g):
    B, S, D = x.shape
    ts = _tile(S, NORM_ROWS)
    row = pl.BlockSpec((None, ts, D), lambda b, i: (b, i, 0))
    return pl.pallas_call(
        _final_norm_kernel,
        grid=(B, S // ts),
        in_specs=[row, pl.BlockSpec((1, D), lambda b, i: (0, 0))],
        out_specs=row,
        out_shape=jax.ShapeDtypeStruct((B, S, D), F32),
        compiler_params=_params(VMEM_LIMIT_BIG, ("parallel", "parallel")),
        name="final_norm",
    )(x, g.reshape(1, D))


def _rope_table_kernel(inv_ref, cos_ref, sin_ref):
    ts = cos_ref.shape[0]
    pos = (pl.program_id(0) * ts + lax.broadcasted_iota(jnp.int32, (ts, HEAD_DIM), 0)).astype(F32)
    ang = pos * inv_ref[...]
    lane = lax.broadcasted_iota(jnp.int32, (ts, HEAD_DIM), 1)
    cos_ref[...] = jnp.cos(ang)
    sin_ref[...] = jnp.where(lane < HEAD_DIM // 2, -jnp.sin(ang), jnp.sin(ang))


def _rope_tables(S):
    half = HEAD_DIM // 2
    inv = ROPE_THETA ** (-jnp.arange(half, dtype=F32) / half)
    inv2 = jnp.concatenate([inv, inv]).reshape(1, HEAD_DIM)
    ts = _tile(S, 512)
    tab = pl.BlockSpec((ts, HEAD_DIM), lambda i: (i, 0))
    return pl.pallas_call(
        _rope_table_kernel,
        grid=(S // ts,),
        in_specs=[pl.BlockSpec((1, HEAD_DIM), lambda i: (0, 0))],
        out_specs=[tab, tab],
        out_shape=[jax.ShapeDtypeStruct((S, HEAD_DIM), F32)] * 2,
        name="rope_tables",
    )(inv2)


def _proj_kernel(*refs, n_lead, lead_scale, rope, w_rows):
    if rope:
        x_ref, w_ref, cos_ref, sin_ref, o_ref = refs
    else:
        x_ref, w_ref, o_ref = refs
    contract = (((1,), (1 if w_rows else 0,)), ((), ()))
    acc = lax.dot_general(x_ref[...], w_ref[...], contract, preferred_element_type=F32)
    tn = acc.shape[1]
    j = pl.program_id(1)

    def lead():
        cos, sin = cos_ref[...], sin_ref[...]
        for h in range(tn // HEAD_DIM):
            sl = slice(h * HEAD_DIM, (h + 1) * HEAD_DIM)
            xh = acc[:, sl]
            r = xh * cos + pltpu.roll(xh, HEAD_DIM // 2, axis=1) * sin
            o_ref[:, sl] = r.astype(o_ref.dtype)

    def rest():
        o_ref[...] = acc.astype(o_ref.dtype)

    if n_lead == 0:
        rest()
    elif rope:
        pl.when(j < n_lead)(lead)
        pl.when(j >= n_lead)(rest)
    else:
        scale = jnp.where(j < n_lead, lead_scale, 1.0).astype(F32)
        o_ref[...] = (acc * scale).astype(o_ref.dtype)


def _proj(x, w, *, w_rows=False, lead_cols=0, lead_scale=1.0, rope_tabs=None, seq=None,
          tn_pref=1024):
    M, K = x.shape
    N = w.shape[0] if w_rows else w.shape[1]
    tm = _tile(M if seq is None else seq, 1024)
    tn = _tile(N, tn_pref)
    assert lead_cols % tn == 0 and tn % HEAD_DIM == 0
    w_spec = (pl.BlockSpec((tn, K), lambda i, j: (j, 0)) if w_rows
              else pl.BlockSpec((K, tn), lambda i, j: (0, j)))
    in_specs = [pl.BlockSpec((tm, K), lambda i, j: (i, 0)), w_spec]
    args = [x, w]
    if rope_tabs is not None:
        spt = seq // tm
        tab = pl.BlockSpec((tm, HEAD_DIM), lambda i, j: (i % spt, 0))
        in_specs += [tab, tab]
        args += [t * lead_scale for t in rope_tabs] if lead_scale != 1.0 else list(rope_tabs)
    return pl.pallas_call(
        functools.partial(_proj_kernel, n_lead=lead_cols // tn, lead_scale=lead_scale,
                          rope=rope_tabs is not None, w_rows=w_rows),
        grid=(M // tm, N // tn),
        in_specs=in_specs,
        out_specs=pl.BlockSpec((tm, tn), lambda i, j: (i, j)),
        out_shape=jax.ShapeDtypeStruct((M, N), BF16),
        compiler_params=_params(VMEM_LIMIT_BIG, ("parallel", "parallel")),
        name="proj",
    )(*args)


def _cast_kernel(w_ref, o_ref):
    o_ref[...] = w_ref[...].astype(o_ref.dtype)


def _cast_rows(w_t, layer, n_rows):
    K = w_t.shape[2]
    tr = _tile(n_rows, 512)
    return pl.pallas_call(
        _cast_kernel,
        grid=(n_rows // tr,),
        in_specs=[pl.BlockSpec((None, tr, K), lambda i: (layer, i, 0))],
        out_specs=pl.BlockSpec((tr, K), lambda i: (i, 0)),
        out_shape=jax.ShapeDtypeStruct((n_rows, K), BF16),
        compiler_params=_params(VMEM_LIMIT_MID, ("parallel",)),
        name="cast_rows",
    )(w_t)


def _out_proj_kernel(a_ref, w_ref, x_ref, gate_ref, o_ref):
    y = jnp.dot(a_ref[...], w_ref[...], preferred_element_type=F32)
    o_ref[...] = x_ref[...] + gate_ref[...] * y


def _out_proj(a, w, x, mod3):
    B, S, N = x.shape
    K = a.shape[1]
    tm = _tile(S, 1024)
    tn = _tile(N, 512)
    spt = S // tm
    nj = N // tn
    out = pl.pallas_call(
        _out_proj_kernel,
        grid=(B * spt, nj),
        in_specs=[
            pl.BlockSpec((tm, K), lambda i, j: (i, 0)),
            pl.BlockSpec((K, tn), lambda i, j: (0, j)),
            pl.BlockSpec((tm, tn), lambda i, j: (i, j)),
            pl.BlockSpec((None, 1, tn), lambda i, j: (i // spt, 0, 2 * nj + j)),
        ],
        out_specs=pl.BlockSpec((tm, tn), lambda i, j: (i, j)),
        out_shape=jax.ShapeDtypeStruct((B * S, N), F32),
        compiler_params=_params(VMEM_LIMIT_BIG, ("parallel", "parallel")),
        name="out_proj",
    )(a, w, x.reshape(B * S, N), mod3)
    return out.reshape(B, S, N)


def _split3(x):
    hi = x.astype(BF16)
    r1 = x - hi.astype(F32)
    mid = r1.astype(BF16)
    lo = (r1 - mid.astype(F32)).astype(BF16)
    return hi, mid, lo


def _forget_kernel(h_ref, w_ref, b_ref, aq_ref, ak_ref, carry_ref, *, n_heads):
    ts = h_ref.shape[0]

    @pl.when(pl.program_id(1) == 0)
    def _():
        carry_ref[...] = jnp.zeros_like(carry_ref)

    w_t = w_ref[...].astype(BF16)
    w_t = jnp.concatenate([w_t, jnp.zeros((LANES - w_t.shape[0], w_t.shape[1]), BF16)], axis=0)
    f = lax.dot_general(h_ref[...], w_t, (((1,), (1,)), ((), ())),
                        preferred_element_type=F32) + b_ref[...]
    log_f = jnp.minimum(f, 0.0) - jnp.log1p(jnp.exp(-jnp.abs(f)))
    row = lax.broadcasted_iota(jnp.int32, (ts, ts), 0)
    col = lax.broadcasted_iota(jnp.int32, (ts, ts), 1)
    tri = jnp.where(row >= col, 1.0, 0.0).astype(BF16)
    cum = carry_ref[...]
    for part in _split3(log_f):
        cum = cum + jnp.dot(tri, part, preferred_element_type=F32)
    carry_ref[...] = cum[ts - 1:ts, :]

    hi, mid, lo = (p.astype(F32) for p in _split3(cum * LOG2E))
    lane = lax.broadcasted_iota(jnp.int32, (ts, LANES), 1)
    ones_q = jnp.where((lane >= 3) & (lane < 6), 1.0, 0.0)
    ones_k = jnp.where(lane < 3, 1.0, 0.0)
    for h in range(n_heads):
        c = slice(h, h + 1)
        fq = jnp.where(lane == 0, hi[:, c], jnp.where(lane == 1, mid[:, c],
                       jnp.where(lane == 2, lo[:, c], ones_q)))
        fk = jnp.where(lane == 3, -hi[:, c], jnp.where(lane == 4, -mid[:, c],
                       jnp.where(lane == 5, -lo[:, c], ones_k)))
        aq_ref[h] = fq.astype(BF16)
        ak_ref[h] = fk.astype(BF16)


def _forget_columns(h, w_t, layer, row0, b_f):
    B, S, D = h.shape
    H = b_f.shape[0]
    assert H <= LANES and H % BF16_SUBLANES == 0 and row0 % H == 0
    b_pad = jnp.zeros((1, LANES), F32).at[0, :H].set(b_f.astype(F32))
    ts = _tile(S, 256)
    out = pl.BlockSpec((None, H, ts, LANES), lambda b, i: (b, 0, i, 0))
    return pl.pallas_call(
        functools.partial(_forget_kernel, n_heads=H),
        grid=(B, S // ts),
        in_specs=[
            pl.BlockSpec((None, ts, D), lambda b, i: (b, i, 0)),
            pl.BlockSpec((None, H, D), lambda b, i: (layer, row0 // H, 0)),
            pl.BlockSpec((1, LANES), lambda b, i: (0, 0)),
        ],
        out_specs=[out, out],
        out_shape=[jax.ShapeDtypeStruct((B, H, S, LANES), BF16)] * 2,
        scratch_shapes=[pltpu.VMEM((1, LANES), F32)],
        compiler_params=_params(VMEM_LIMIT_MID, ("parallel", "arbitrary")),
        name="forget_columns",
    )(h, w_t, b_pad)


def _fox_kernel(q_ref, k_ref, v_ref, z_ref, aq_ref, ak_ref, o_ref, vt_ref, *, tq, hp):
    S = q_ref.shape[0]
    nq = S // tq
    hd = HEAD_DIM
    ones = jnp.ones((VT_ROWS - hd, tq), BF16)
    for g in range(hp):
        for j in range(nq):
            vblk = v_ref[j * tq:(j + 1) * tq, g * hd:(g + 1) * hd].astype(F32)
            vt_ref[g, j, 0:hd, :] = vblk.T.astype(BF16)
            vt_ref[g, j, hd:VT_ROWS, :] = ones

    key = lax.broadcasted_iota(jnp.int32, (tq, tq), 0)
    qry = lax.broadcasted_iota(jnp.int32, (tq, tq), 1)
    causal_t = key <= qry

    def scores(g, qa, ks, masked):
        ka = jnp.concatenate([k_ref[pl.ds(ks, tq), g * hd:(g + 1) * hd],
                              ak_ref[g, pl.ds(ks, tq), :]], axis=1)
        s_t = lax.dot_general(ka, qa, (((1,), (1,)), ((), ())), preferred_element_type=F32)
        return jnp.where(causal_t, s_t, -jnp.inf) if masked else s_t

    def update(g, j, s_t, carry):
        m, acc = carry
        m_new = jnp.maximum(m, jnp.max(s_t, axis=0, keepdims=True))
        alpha = jnp.exp2(m - m_new)
        p_t = jnp.exp2(s_t - m_new).astype(BF16)
        acc = alpha * acc + jnp.dot(vt_ref[g, j], p_t, preferred_element_type=F32)
        return m_new, acc

    def steps(qas, j, ks, carry, masked):
        s_ts = [scores(g, qas[g], ks, masked) for g in range(hp)]
        return tuple(update(g, j, s_ts[g], carry[g]) for g in range(hp))

    for i in range(nq):
        qs = i * tq
        qas = [jnp.concatenate([q_ref[qs:qs + tq, g * hd:(g + 1) * hd], aq_ref[g, qs:qs + tq, :]],
                               axis=1) for g in range(hp)]
        carry = tuple((jnp.full((1, tq), -jnp.inf, F32), jnp.zeros((VT_ROWS, tq), F32))
                      for _ in range(hp))
        s_cur = [scores(g, qas[g], 0, i == 0) for g in range(hp)]
        for j in range(i + 1):
            if j < i:
                s_nxt = [scores(g, qas[g], (j + 1) * tq, j + 1 == i) for g in range(hp)]
            carry = tuple(update(g, j, s_cur[g], carry[g]) for g in range(hp))
            if j < i:
                s_cur = s_nxt
        for g in range(hp):
            acc = carry[g][1]
            o = (acc[0:hd, :] * (1.0 / acc[hd:hd + 1, :])).T
            z = z_ref[qs:qs + tq, g * hd:(g + 1) * hd].astype(F32)
            o_ref[qs:qs + tq, g * hd:(g + 1) * hd] = (
                o * (z * jax.nn.sigmoid(z))).astype(o_ref.dtype)


def _fox_attention(qkvz, aq, ak, n_heads):
    B, S, _ = qkvz.shape
    H = n_heads
    hp = 2 if H % 2 == 0 else 1
    tq = _tile(S, 512)
    w = hp * HEAD_DIM
    col = lambda c: pl.BlockSpec((None, S, w), lambda b, h: (b, 0, c * (H // hp) + h))
    side = pl.BlockSpec((None, hp, S, LANES), lambda b, h: (b, h, 0, 0))
    return pl.pallas_call(
        functools.partial(_fox_kernel, tq=tq, hp=hp),
        grid=(B, H // hp),
        in_specs=[col(0), col(1), col(2), col(3), side, side],
        out_specs=pl.BlockSpec((None, S, w), lambda b, h: (b, 0, h)),
        out_shape=jax.ShapeDtypeStruct((B, S, H * HEAD_DIM), BF16),
        scratch_shapes=[pltpu.VMEM((hp, S // tq, VT_ROWS, tq), BF16)],
        compiler_params=_params(VMEM_LIMIT_BIG, ("parallel", "parallel")),
        name="fox_attention",
    )(qkvz, qkvz, qkvz, qkvz, aq, ak)


def _swa_kernel(q_ref, z_ref, k_ref, v_ref, sink_ref, o_ref, *, group):
    tq = q_ref.shape[0]
    blk = WINDOW
    hd = HEAD_DIM
    base = pl.program_id(2) * tq
    kk = lax.broadcasted_iota(jnp.int32, (2 * blk, blk), 0)
    qi = lax.broadcasted_iota(jnp.int32, (2 * blk, blk), 1)
    sink = sink_ref[...] * LOG2E
    ones = jnp.ones((VT_ROWS - hd, 2 * blk), BF16)
    for nb in range(tq // blk):
        start = base + nb * blk
        kstart = pl.multiple_of(jnp.maximum(start - blk, 0), blk)
        diff = qi + (start - kstart) - kk
        bias1 = jnp.where((diff >= 0) & (diff < WINDOW), 0.0, -jnp.inf)
        bias = jnp.concatenate([bias1] * group, axis=1)
        kb = k_ref[pl.ds(kstart, 2 * blk), :]
        vb = v_ref[pl.ds(kstart, 2 * blk), :].astype(F32)
        vt = jnp.concatenate([vb.T.astype(BF16), ones], axis=0)
        qt = q_ref[nb * blk:(nb + 1) * blk, :]
        qs = jnp.concatenate([qt[:, g * hd:(g + 1) * hd] for g in range(group)], axis=0)
        s_t = lax.dot_general(kb, qs, (((1,), (1,)), ((), ())), preferred_element_type=F32) + bias
        m = jnp.maximum(jnp.max(s_t, axis=0, keepdims=True), sink)
        p_t = jnp.exp2(s_t - m).astype(BF16)
        acc = jnp.dot(vt, p_t, preferred_element_type=F32)
        denom = acc[hd:hd + 1, :] + jnp.exp2(sink - m)
        o_t = acc[0:hd, :] * (1.0 / denom)
        for g in range(group):
            sl = slice(g * hd, (g + 1) * hd)
            z = z_ref[nb * blk:(nb + 1) * blk, sl].astype(F32)
            o_ref[nb * blk:(nb + 1) * blk, sl] = (
                o_t[:, sl].T * (z * jax.nn.sigmoid(z))).astype(o_ref.dtype)


def _swa_attention(qz, kv, sinks):
    B, S, W2 = qz.shape
    width = W2 // 2
    hq = width // HEAD_DIM
    group = hq // KV_HEADS
    gw = group * HEAD_DIM
    tq = _tile(S, 512)
    sink_b = jnp.broadcast_to(sinks.astype(F32).reshape(KV_HEADS, 1, group, 1),
                              (KV_HEADS, 1, group, LANES)).reshape(KV_HEADS, 1, group * LANES)
    return pl.pallas_call(
        functools.partial(_swa_kernel, group=group),
        grid=(B, KV_HEADS, S // tq),
        in_specs=[
            pl.BlockSpec((None, tq, gw), lambda b, g, i: (b, i, g)),
            pl.BlockSpec((None, tq, gw), lambda b, g, i: (b, i, KV_HEADS + g)),
            pl.BlockSpec((None, S, HEAD_DIM), lambda b, g, i: (b, 0, g)),
            pl.BlockSpec((None, S, HEAD_DIM), lambda b, g, i: (b, 0, KV_HEADS + g)),
            pl.BlockSpec((None, 1, group * LANES), lambda b, g, i: (g, 0, 0)),
        ],
        out_specs=pl.BlockSpec((None, tq, gw), lambda b, g, i: (b, i, g)),
        out_shape=jax.ShapeDtypeStruct((B, S, width), BF16),
        compiler_params=_params(VMEM_LIMIT_MID, ("parallel", "parallel", "parallel")),
        name="swa_attention",
    )(qz, qz, kv, kv, sink_b)


def kernel(x, c, norm_g, ada_w, ada_b, a_w_in, a_b_f, a_w_out, kv_norm_g, kv_w,
           b_w_in, b_sinks, b_w_out, final_norm_g):
    B, S, D = x.shape
    n_a = a_w_in.shape[0]
    n_b = b_w_in.shape[0]
    depth = n_a + n_b
    sm_scale = HEAD_DIM ** -0.5 * LOG2E

    mod = _adaln(c, ada_w, ada_b)
    rope_tabs = _rope_tables(S)

    h = x
    kv = None
    for layer in range(depth):
        mod3 = mod[layer].reshape(B, 1, 3 * D)
        if layer < n_a:
            a_heads = a_b_f.shape[1]
            a_width = a_heads * HEAD_DIM
            hn = _norm_mod(h, norm_g[layer], mod3)
            w_in_t = jnp.swapaxes(a_w_in, 1, 2)
            qkvz = _proj(hn.reshape(B * S, D), _cast_rows(w_in_t, layer, 4 * a_width),
                         w_rows=True, lead_cols=a_width, lead_scale=sm_scale)
            aq, ak = _forget_columns(hn, w_in_t, layer, 4 * a_width, a_b_f[layer])
            o = _fox_attention(qkvz.reshape(B, S, 4 * a_width), aq, ak, a_heads)
            h = _out_proj(o.reshape(B * S, a_width), a_w_out[layer].astype(BF16), h, mod3)
        else:
            j = layer - n_a
            if kv is None:
                hkv, hn = _norm_dual(h, kv_norm_g, norm_g[layer], mod3)
                kvw = kv_w.shape[1] // 2
                kv = _proj(hkv.reshape(B * S, D), kv_w.astype(BF16), lead_cols=kvw,
                           rope_tabs=rope_tabs, seq=S, tn_pref=kvw).reshape(B, S, 2 * kvw)
            else:
                hn = _norm_mod(h, norm_g[layer], mod3)
            b_width = b_w_in.shape[2] // 2
            qz = _proj(hn.reshape(B * S, D), b_w_in[j].astype(BF16), lead_cols=b_width,
                       lead_scale=sm_scale, rope_tabs=rope_tabs, seq=S)
            o = _swa_attention(qz.reshape(B, S, 2 * b_width), kv, b_sinks[j])
            h = _out_proj(o.reshape(B * S, b_width), b_w_out[j].astype(BF16), h, mod3)
    return _final_norm(h, final_norm_g)
```

```python
import functools
import math

import jax
import jax.numpy as jnp
import numpy as np
from jax import lax
from jax.experimental import pallas as pl
from jax.experimental.pallas import tpu as pltpu

HEAD_DIM = 128
KV_HEADS = 4
WINDOW = 128
ROPE_THETA = 10000.0
EPS = 1e-6

V7X_VMEM_BYTES = 64 * 1024 * 1024
VMEM_LIMIT_BIG = 56 * 1024 * 1024
VMEM_LIMIT_MID = 40 * 1024 * 1024
LANES = 128
SUBLANES = 8
BF16_SUBLANES = 16
VT_ROWS = HEAD_DIM + BF16_SUBLANES
LOG2E = math.log2(math.e)
NORM_ROWS = 512

F32 = jnp.float32
BF16 = jnp.bfloat16


def _params(limit, sem):
    return pltpu.CompilerParams(dimension_semantics=sem, vmem_limit_bytes=limit)


def _tile(n, pref):
    t = min(n, pref)
    assert n % t == 0, (n, t)
    return t


def _adaln_kernel(c_ref, w_ref, b_ref, o_ref):
    c = c_ref[...]
    sc = (c * jax.nn.sigmoid(c)).astype(BF16)
    acc = jnp.dot(sc, w_ref[...].astype(BF16), preferred_element_type=F32)
    o_ref[...] = acc + b_ref[...]


def _adaln(c, ada_w, ada_b):
    L, D, N = ada_w.shape
    B = c.shape[0]
    bp = -(-B // SUBLANES) * SUBLANES
    c_pad = jnp.zeros((bp, D), F32).at[:B].set(c)
    tn = _tile(N, 512)
    out = pl.pallas_call(
        _adaln_kernel,
        grid=(L, N // tn),
        in_specs=[
            pl.BlockSpec((bp, D), lambda l, j: (0, 0)),
            pl.BlockSpec((None, D, tn), lambda l, j: (l, 0, j)),
            pl.BlockSpec((None, 1, tn), lambda l, j: (l, 0, j)),
        ],
        out_specs=pl.BlockSpec((None, bp, tn), lambda l, j: (l, 0, j)),
        out_shape=jax.ShapeDtypeStruct((L, bp, N), F32),
        compiler_params=_params(VMEM_LIMIT_MID, ("parallel", "parallel")),
        name="adaln",
    )(c_pad, ada_w, ada_b.reshape(L, 1, N))
    return out[:, :B]


def _rms(x):
    return x * lax.rsqrt(jnp.mean(x * x, axis=-1, keepdims=True) + EPS)


def _norm_mod_kernel(x_ref, g_ref, shift_ref, scale_ref, o_ref):
    y = _rms(x_ref[...]) * g_ref[...]
    o_ref[...] = (y * (1.0 + scale_ref[...]) + shift_ref[...]).astype(o_ref.dtype)


def _norm_mod(x, g, mod3):
    B, S, D = x.shape
    ts = _tile(S, NORM_ROWS)
    return pl.pallas_call(
        _norm_mod_kernel,
        grid=(B, S // ts),
        in_specs=[
            pl.BlockSpec((None, ts, D), lambda b, i: (b, i, 0)),
            pl.BlockSpec((1, D), lambda b, i: (0, 0)),
            pl.BlockSpec((None, 1, D), lambda b, i: (b, 0, 0)),
            pl.BlockSpec((None, 1, D), lambda b, i: (b, 0, 1)),
        ],
        out_specs=pl.BlockSpec((None, ts, D), lambda b, i: (b, i, 0)),
        out_shape=jax.ShapeDtypeStruct((B, S, D), BF16),
        compiler_params=_params(VMEM_LIMIT_MID, ("parallel", "parallel")),
        name="norm_mod",
    )(x, g.reshape(1, D), mod3, mod3)


def _norm_dual_kernel(x_ref, gkv_ref, g_ref, shift_ref, scale_ref, okv_ref, o_ref):
    y = _rms(x_ref[...])
    okv_ref[...] = (y * gkv_ref[...]).astype(okv_ref.dtype)
    o_ref[...] = ((y * g_ref[...]) * (1.0 + scale_ref[...]) + shift_ref[...]).astype(o_ref.dtype)


def _norm_dual(x, g_kv, g, mod3):
    B, S, D = x.shape
    ts = _tile(S, NORM_ROWS)
    row = pl.BlockSpec((None, ts, D), lambda b, i: (b, i, 0))
    vec = pl.BlockSpec((1, D), lambda b, i: (0, 0))
    return pl.pallas_call(
        _norm_dual_kernel,
        grid=(B, S // ts),
        in_specs=[row, vec, vec,
                  pl.BlockSpec((None, 1, D), lambda b, i: (b, 0, 0)),
                  pl.BlockSpec((None, 1, D), lambda b, i: (b, 0, 1))],
        out_specs=[row, row],
        out_shape=[jax.ShapeDtypeStruct((B, S, D), BF16)] * 2,
        compiler_params=_params(VMEM_LIMIT_BIG, ("parallel", "parallel")),
        name="norm_dual",
    )(x, g_kv.reshape(1, D), g.reshape(1, D), mod3, mod3)


def _final_norm_kernel(x_ref, g_ref, o_ref):
    o_ref[...] = _rms(x_ref[...]) * g_ref[...]


def _final_norm(x, g):
    B, S, D = x.shape
    ts = _tile(S, NORM_ROWS)
    row = pl.BlockSpec((None, ts, D), lambda b, i: (b, i, 0))
    return pl.pallas_call(
        _final_norm_kernel,
        grid=(B, S // ts),
        in_specs=[row, pl.BlockSpec((1, D), lambda b, i: (0, 0))],
        out_specs=row,
        out_shape=jax.ShapeDtypeStruct((B, S, D), F32),
        compiler_params=_params(VMEM_LIMIT_BIG, ("parallel", "parallel")),
        name="final_norm",
    )(x, g.reshape(1, D))


def _rope_table_kernel(inv_ref, cos_ref, sin_ref):
    ts = cos_ref.shape[0]
    pos = (pl.program_id(0) * ts + lax.broadcasted_iota(jnp.int32, (ts, HEAD_DIM), 0)).astype(F32)
    ang = pos * inv_ref[...]
    lane = lax.broadcasted_iota(jnp.int32, (ts, HEAD_DIM), 1)
    cos_ref[...] = jnp.cos(ang)
    sin_ref[...] = jnp.where(lane < HEAD_DIM // 2, -jnp.sin(ang), jnp.sin(ang))


def _rope_tables(S):
    half = HEAD_DIM // 2
    inv = ROPE_THETA ** (-jnp.arange(half, dtype=F32) / half)
    inv2 = jnp.concatenate([inv, inv]).reshape(1, HEAD_DIM)
    ts = _tile(S, 512)
    tab = pl.BlockSpec((ts, HEAD_DIM), lambda i: (i, 0))
    return pl.pallas_call(
        _rope_table_kernel,
        grid=(S // ts,),
        in_specs=[pl.BlockSpec((1, HEAD_DIM), lambda i: (0, 0))],
        out_specs=[tab, tab],
        out_shape=[jax.ShapeDtypeStruct((S, HEAD_DIM), F32)] * 2,
        name="rope_tables",
    )(inv2)


def _proj_kernel(*refs, n_lead, lead_scale, rope, w_rows):
    if rope:
        x_ref, w_ref, cos_ref, sin_ref, o_ref = refs
    else:
        x_ref, w_ref, o_ref = refs
    contract = (((1,), (1 if w_rows else 0,)), ((), ()))
    acc = lax.dot_general(x_ref[...], w_ref[...], contract, preferred_element_type=F32)
    tn = acc.shape[1]
    j = pl.program_id(1)

    def lead():
        cos, sin = cos_ref[...], sin_ref[...]
        for h in range(tn // HEAD_DIM):
            sl = slice(h * HEAD_DIM, (h + 1) * HEAD_DIM)
            xh = acc[:, sl]
            r = xh * cos + pltpu.roll(xh, HEAD_DIM // 2, axis=1) * sin
            o_ref[:, sl] = r.astype(o_ref.dtype)

    def rest():
        o_ref[...] = acc.astype(o_ref.dtype)

    if n_lead == 0:
        rest()
    elif rope:
        pl.when(j < n_lead)(lead)
        pl.when(j >= n_lead)(rest)
    else:
        scale = jnp.where(j < n_lead, lead_scale, 1.0).astype(F32)
        o_ref[...] = (acc * scale).astype(o_ref.dtype)


def _proj(x, w, *, w_rows=False, lead_cols=0, lead_scale=1.0, rope_tabs=None, seq=None,
          tn_pref=1024):
    M, K = x.shape
    N = w.shape[0] if w_rows else w.shape[1]
    tm = _tile(M if seq is None else seq, 1024)
    tn = _tile(N, tn_pref)
    assert lead_cols % tn == 0 and tn % HEAD_DIM == 0
    w_spec = (pl.BlockSpec((tn, K), lambda i, j: (j, 0)) if w_rows
              else pl.BlockSpec((K, tn), lambda i, j: (0, j)))
    in_specs = [pl.BlockSpec((tm, K), lambda i, j: (i, 0)), w_spec]
    args = [x, w]
    if rope_tabs is not None:
        spt = seq // tm
        tab = pl.BlockSpec((tm, HEAD_DIM), lambda i, j: (i % spt, 0))
        in_specs += [tab, tab]
        args += [t * lead_scale for t in rope_tabs] if lead_scale != 1.0 else list(rope_tabs)
    return pl.pallas_call(
        functools.partial(_proj_kernel, n_lead=lead_cols // tn, lead_scale=lead_scale,
                          rope=rope_tabs is not None, w_rows=w_rows),
        grid=(M // tm, N // tn),
        in_specs=in_specs,
        out_specs=pl.BlockSpec((tm, tn), lambda i, j: (i, j)),
        out_shape=jax.ShapeDtypeStruct((M, N), BF16),
        compiler_params=_params(VMEM_LIMIT_BIG, ("parallel", "parallel")),
        name="proj",
    )(*args)


def _cast_kernel(w_ref, o_ref):
    o_ref[...] = w_ref[...].astype(o_ref.dtype)


def _cast_rows(w_t, layer, n_rows):
    K = w_t.shape[2]
    tr = _tile(n_rows, 512)
    return pl.pallas_call(
        _cast_kernel,
        grid=(n_rows // tr,),
        in_specs=[pl.BlockSpec((None, tr, K), lambda i: (layer, i, 0))],
        out_specs=pl.BlockSpec((tr, K), lambda i: (i, 0)),
        out_shape=jax.ShapeDtypeStruct((n_rows, K), BF16),
        compiler_params=_params(VMEM_LIMIT_MID, ("parallel",)),
        name="cast_rows",
    )(w_t)


def _out_proj_kernel(a_ref, w_ref, x_ref, gate_ref, o_ref):
    y = jnp.dot(a_ref[...], w_ref[...], preferred_element_type=F32)
    o_ref[...] = x_ref[...] + gate_ref[...] * y


def _out_proj(a, w, x, mod3):
    B, S, N = x.shape
    K = a.shape[1]
    tm = _tile(S, 1024)
    tn = _tile(N, 512)
    spt = S // tm
    nj = N // tn
    out = pl.pallas_call(
        _out_proj_kernel,
        grid=(B * spt, nj),
        in_specs=[
            pl.BlockSpec((tm, K), lambda i, j: (i, 0)),
            pl.BlockSpec((K, tn), lambda i, j: (0, j)),
            pl.BlockSpec((tm, tn), lambda i, j: (i, j)),
            pl.BlockSpec((None, 1, tn), lambda i, j: (i // spt, 0, 2 * nj + j)),
        ],
        out_specs=pl.BlockSpec((tm, tn), lambda i, j: (i, j)),
        out_shape=jax.ShapeDtypeStruct((B * S, N), F32),
        compiler_params=_params(VMEM_LIMIT_BIG, ("parallel", "parallel")),
        name="out_proj",
    )(a, w, x.reshape(B * S, N), mod3)
    return out.reshape(B, S, N)


def _split3(x):
    hi = x.astype(BF16)
    r1 = x - hi.astype(F32)
    mid = r1.astype(BF16)
    lo = (r1 - mid.astype(F32)).astype(BF16)
    return hi, mid, lo


def _forget_kernel(h_ref, w_ref, b_ref, perm_ref, sel_ref, aq_ref, ak_ref, carry_ref, *, n_heads):
    ts = h_ref.shape[0]

    @pl.when(pl.program_id(1) == 0)
    def _():
        carry_ref[...] = jnp.zeros_like(carry_ref)

    w_t = w_ref[...].astype(BF16)
    w_t = jnp.concatenate([w_t, jnp.zeros((LANES - w_t.shape[0], w_t.shape[1]), BF16)], axis=0)
    f = lax.dot_general(h_ref[...], w_t, (((1,), (1,)), ((), ())),
                        preferred_element_type=F32) + b_ref[...]
    log_f = jnp.minimum(f, 0.0) - jnp.log1p(jnp.exp(-jnp.abs(f)))
    row = lax.broadcasted_iota(jnp.int32, (ts, ts), 0)
    col = lax.broadcasted_iota(jnp.int32, (ts, ts), 1)
    tri = jnp.where(row >= col, 1.0, 0.0).astype(BF16)
    cum = carry_ref[...]
    for part in _split3(log_f):
        cum = cum + jnp.dot(tri, part, preferred_element_type=F32)
    carry_ref[...] = cum[ts - 1:ts, :]

    parts = list(_split3(cum * LOG2E)) + [jnp.ones((ts, LANES), BF16)]
    packed = jnp.dot(jnp.concatenate(parts, axis=1), perm_ref[...],
                     preferred_element_type=F32).astype(BF16)
    for h in range(n_heads):
        r = jnp.dot(packed, sel_ref[h], preferred_element_type=F32)
        aq_ref[h] = r[:, :LANES].astype(BF16)
        ak_ref[h] = r[:, LANES:].astype(BF16)


def _side_column_selectors(n_heads):
    one = 3 * n_heads
    assert one < LANES
    perm = np.zeros((4 * LANES, LANES), np.float32)
    sel = np.zeros((n_heads, LANES, 2 * LANES), np.float32)
    perm[3 * LANES, one] = 1.0
    for h in range(n_heads):
        for p in range(3):
            perm[p * LANES + h, 3 * h + p] = 1.0
            sel[h, 3 * h + p, p] = 1.0
            sel[h, 3 * h + p, LANES + 3 + p] = -1.0
            sel[h, one, 3 + p] = 1.0
            sel[h, one, LANES + p] = 1.0
    return jnp.asarray(perm, BF16), jnp.asarray(sel, BF16)


def _forget_columns(h, w_t, layer, row0, b_f):
    B, S, D = h.shape
    H = b_f.shape[0]
    assert H <= LANES and H % BF16_SUBLANES == 0 and row0 % H == 0
    b_pad = jnp.zeros((1, LANES), F32).at[0, :H].set(b_f.astype(F32))
    ts = _tile(S, 256)
    out = pl.BlockSpec((None, H, ts, LANES), lambda b, i: (b, 0, i, 0))
    return pl.pallas_call(
        functools.partial(_forget_kernel, n_heads=H),
        grid=(B, S // ts),
        in_specs=[
            pl.BlockSpec((None, ts, D), lambda b, i: (b, i, 0)),
            pl.BlockSpec((None, H, D), lambda b, i: (layer, row0 // H, 0)),
            pl.BlockSpec((1, LANES), lambda b, i: (0, 0)),
            pl.BlockSpec((4 * LANES, LANES), lambda b, i: (0, 0)),
            pl.BlockSpec((H, LANES, 2 * LANES), lambda b, i: (0, 0, 0)),
        ],
        out_specs=[out, out],
        out_shape=[jax.ShapeDtypeStruct((B, H, S, LANES), BF16)] * 2,
        scratch_shapes=[pltpu.VMEM((1, LANES), F32)],
        compiler_params=_params(VMEM_LIMIT_MID, ("parallel", "arbitrary")),
        name="forget_columns",
    )(h, w_t, b_pad, *_side_column_selectors(H))


def _fox_kernel(q_ref, k_ref, v_ref, z_ref, aq_ref, ak_ref, o_ref, vt_ref, *, tq, hp):
    S = q_ref.shape[0]
    nq = S // tq
    hd = HEAD_DIM
    ones = jnp.ones((VT_ROWS - hd, tq), BF16)
    for g in range(hp):
        for j in range(nq):
            vblk = v_ref[j * tq:(j + 1) * tq, g * hd:(g + 1) * hd].astype(F32)
            vt_ref[g, j, 0:hd, :] = vblk.T.astype(BF16)
            vt_ref[g, j, hd:VT_ROWS, :] = ones

    key = lax.broadcasted_iota(jnp.int32, (tq, tq), 0)
    qry = lax.broadcasted_iota(jnp.int32, (tq, tq), 1)
    causal_t = key <= qry

    def scores(g, qa, ks, masked):
        ka = jnp.concatenate([k_ref[pl.ds(ks, tq), g * hd:(g + 1) * hd],
                              ak_ref[g, pl.ds(ks, tq), :]], axis=1)
        s_t = lax.dot_general(ka, qa, (((1,), (1,)), ((), ())), preferred_element_type=F32)
        return jnp.where(causal_t, s_t, -jnp.inf) if masked else s_t

    def update(g, j, s_t, carry):
        m, acc = carry
        m_new = jnp.maximum(m, jnp.max(s_t, axis=0, keepdims=True))
        alpha = jnp.exp2(m - m_new)
        p_t = jnp.exp2(s_t - m_new).astype(BF16)
        acc = alpha * acc + jnp.dot(vt_ref[g, j], p_t, preferred_element_type=F32)
        return m_new, acc

    def steps(qas, j, ks, carry, masked):
        s_ts = [scores(g, qas[g], ks, masked) for g in range(hp)]
        return tuple(update(g, j, s_ts[g], carry[g]) for g in range(hp))

    for i in range(nq):
        qs = i * tq
        qas = [jnp.concatenate([q_ref[qs:qs + tq, g * hd:(g + 1) * hd], aq_ref[g, qs:qs + tq, :]],
                               axis=1) for g in range(hp)]
        carry = tuple((jnp.full((1, tq), -jnp.inf, F32), jnp.zeros((VT_ROWS, tq), F32))
                      for _ in range(hp))
        s_cur = [scores(g, qas[g], 0, i == 0) for g in range(hp)]
        for j in range(i + 1):
            if j < i:
                s_nxt = [scores(g, qas[g], (j + 1) * tq, j + 1 == i) for g in range(hp)]
            carry = tuple(update(g, j, s_cur[g], carry[g]) for g in range(hp))
            if j < i:
                s_cur = s_nxt
        for g in range(hp):
            acc = carry[g][1]
            o = (acc[0:hd, :] * (1.0 / acc[hd:hd + 1, :])).T
            z = z_ref[qs:qs + tq, g * hd:(g + 1) * hd].astype(F32)
            o_ref[qs:qs + tq, g * hd:(g + 1) * hd] = (
                o * (z * jax.nn.sigmoid(z))).astype(o_ref.dtype)


def _fox_attention(qkvz, aq, ak, n_heads):
    B, S, _ = qkvz.shape
    H = n_heads
    hp = 2 if H % 2 == 0 else 1
    tq = _tile(S, 512)
    w = hp * HEAD_DIM
    col = lambda c: pl.BlockSpec((None, S, w), lambda b, h: (b, 0, c * (H // hp) + h))
    side = pl.BlockSpec((None, hp, S, LANES), lambda b, h: (b, h, 0, 0))
    return pl.pallas_call(
        functools.partial(_fox_kernel, tq=tq, hp=hp),
        grid=(B, H // hp),
        in_specs=[col(0), col(1), col(2), col(3), side, side],
        out_specs=pl.BlockSpec((None, S, w), lambda b, h: (b, 0, h)),
        out_shape=jax.ShapeDtypeStruct((B, S, H * HEAD_DIM), BF16),
        scratch_shapes=[pltpu.VMEM((hp, S // tq, VT_ROWS, tq), BF16)],
        compiler_params=_params(VMEM_LIMIT_BIG, ("parallel", "parallel")),
        name="fox_attention",
    )(qkvz, qkvz, qkvz, qkvz, aq, ak)


def _swa_kernel(q_ref, z_ref, k_ref, v_ref, sink_ref, o_ref, *, group):
    tq = q_ref.shape[0]
    blk = WINDOW
    hd = HEAD_DIM
    base = pl.program_id(2) * tq
    kk = lax.broadcasted_iota(jnp.int32, (2 * blk, blk), 0)
    qi = lax.broadcasted_iota(jnp.int32, (2 * blk, blk), 1)
    sink = sink_ref[...] * LOG2E
    ones = jnp.ones((VT_ROWS - hd, 2 * blk), BF16)
    for nb in range(tq // blk):
        start = base + nb * blk
        kstart = pl.multiple_of(jnp.maximum(start - blk, 0), blk)
        diff = qi + (start - kstart) - kk
        bias1 = jnp.where((diff >= 0) & (diff < WINDOW), 0.0, -jnp.inf)
        bias = jnp.concatenate([bias1] * group, axis=1)
        kb = k_ref[pl.ds(kstart, 2 * blk), :]
        vb = v_ref[pl.ds(kstart, 2 * blk), :].astype(F32)
        vt = jnp.concatenate([vb.T.astype(BF16), ones], axis=0)
        qt = q_ref[nb * blk:(nb + 1) * blk, :]
        qs = jnp.concatenate([qt[:, g * hd:(g + 1) * hd] for g in range(group)], axis=0)
        s_t = lax.dot_general(kb, qs, (((1,), (1,)), ((), ())), preferred_element_type=F32) + bias
        m = jnp.maximum(jnp.max(s_t, axis=0, keepdims=True), sink)
        p_t = jnp.exp2(s_t - m).astype(BF16)
        acc = jnp.dot(vt, p_t, preferred_element_type=F32)
        denom = acc[hd:hd + 1, :] + jnp.exp2(sink - m)
        o_t = acc[0:hd, :] * (1.0 / denom)
        for g in range(group):
            sl = slice(g * hd, (g + 1) * hd)
            z = z_ref[nb * blk:(nb + 1) * blk, sl].astype(F32)
            o_ref[nb * blk:(nb + 1) * blk, sl] = (
                o_t[:, sl].T * (z * jax.nn.sigmoid(z))).astype(o_ref.dtype)


def _swa_attention(qz, kv, sinks):
    B, S, W2 = qz.shape
    width = W2 // 2
    hq = width // HEAD_DIM
    group = hq // KV_HEADS
    gw = group * HEAD_DIM
    tq = _tile(S, 512)
    sink_b = jnp.broadcast_to(sinks.astype(F32).reshape(KV_HEADS, 1, group, 1),
                              (KV_HEADS, 1, group, LANES)).reshape(KV_HEADS, 1, group * LANES)
    return pl.pallas_call(
        functools.partial(_swa_kernel, group=group),
        grid=(B, KV_HEADS, S // tq),
        in_specs=[
            pl.BlockSpec((None, tq, gw), lambda b, g, i: (b, i, g)),
            pl.BlockSpec((None, tq, gw), lambda b, g, i: (b, i, KV_HEADS + g)),
            pl.BlockSpec((None, S, HEAD_DIM), lambda b, g, i: (b, 0, g)),
            pl.BlockSpec((None, S, HEAD_DIM), lambda b, g, i: (b, 0, KV_HEADS + g)),
            pl.BlockSpec((None, 1, group * LANES), lambda b, g, i: (g, 0, 0)),
        ],
        out_specs=pl.BlockSpec((None, tq, gw), lambda b, g, i: (b, i, g)),
        out_shape=jax.ShapeDtypeStruct((B, S, width), BF16),
        compiler_params=_params(VMEM_LIMIT_MID, ("parallel", "parallel", "parallel")),
        name="swa_attention",
    )(qz, qz, kv, kv, sink_b)


def kernel(x, c, norm_g, ada_w, ada_b, a_w_in, a_b_f, a_w_out, kv_norm_g, kv_w,
           b_w_in, b_sinks, b_w_out, final_norm_g):
    B, S, D = x.shape
    n_a = a_w_in.shape[0]
    n_b = b_w_in.shape[0]
    depth = n_a + n_b
    sm_scale = HEAD_DIM ** -0.5 * LOG2E

    mod = _adaln(c, ada_w, ada_b)
    rope_tabs = _rope_tables(S)

    h = x
    kv = None
    for layer in range(depth):
        mod3 = mod[layer].reshape(B, 1, 3 * D)
        if layer < n_a:
            a_heads = a_b_f.shape[1]
            a_width = a_heads * HEAD_DIM
            hn = _norm_mod(h, norm_g[layer], mod3)
            w_in_t = jnp.swapaxes(a_w_in, 1, 2)
            qkvz = _proj(hn.reshape(B * S, D), _cast_rows(w_in_t, layer, 4 * a_width),
                         w_rows=True, lead_cols=a_width, lead_scale=sm_scale)
            aq, ak = _forget_columns(hn, w_in_t, layer, 4 * a_width, a_b_f[layer])
            o = _fox_attention(qkvz.reshape(B, S, 4 * a_width), aq, ak, a_heads)
            h = _out_proj(o.reshape(B * S, a_width), a_w_out[layer].astype(BF16), h, mod3)
        else:
            j = layer - n_a
            if kv is None:
                hkv, hn = _norm_dual(h, kv_norm_g, norm_g[layer], mod3)
                kvw = kv_w.shape[1] // 2
                kv = _proj(hkv.reshape(B * S, D), kv_w.astype(BF16), lead_cols=kvw,
                           rope_tabs=rope_tabs, seq=S, tn_pref=kvw).reshape(B, S, 2 * kvw)
            else:
                hn = _norm_mod(h, norm_g[layer], mod3)
            b_width = b_w_in.shape[2] // 2
            qz = _proj(hn.reshape(B * S, D), b_w_in[j].astype(BF16), lead_cols=b_width,
                       lead_scale=sm_scale, rope_tabs=rope_tabs, seq=S)
            o = _swa_attention(qz.reshape(B, S, 2 * b_width), kv, b_sinks[j])
            h = _out_proj(o.reshape(B * S, b_width), b_w_out[j].astype(BF16), h, mod3)
    return _final_norm(h, final_norm_g)
```

```python
import functools
import math

import jax
import jax.numpy as jnp
import numpy as np
from jax import lax
from jax.experimental import pallas as pl
from jax.experimental.pallas import tpu as pltpu

HEAD_DIM = 128
KV_HEADS = 4
WINDOW = 128
ROPE_THETA = 10000.0
EPS = 1e-6

V7X_VMEM_BYTES = 64 * 1024 * 1024
VMEM_LIMIT_BIG = 56 * 1024 * 1024
VMEM_LIMIT_MID = 40 * 1024 * 1024
LANES = 128
SUBLANES = 8
BF16_SUBLANES = 16
VT_ROWS = HEAD_DIM + BF16_SUBLANES
LOG2E = math.log2(math.e)
NORM_ROWS = 512

F32 = jnp.float32
BF16 = jnp.bfloat16


def _params(limit, sem):
    return pltpu.CompilerParams(dimension_semantics=sem, vmem_limit_bytes=limit)


def _tile(n, pref):
    t = min(n, pref)
    assert n % t == 0, (n, t)
    return t


def _adaln_kernel(c_ref, w_ref, b_ref, o_ref):
    c = c_ref[...]
    sc = (c * jax.nn.sigmoid(c)).astype(BF16)
    acc = jnp.dot(sc, w_ref[...].astype(BF16), preferred_element_type=F32)
    o_ref[...] = acc + b_ref[...]


def _adaln(c, ada_w, ada_b):
    L, D, N = ada_w.shape
    B = c.shape[0]
    bp = -(-B // SUBLANES) * SUBLANES
    c_pad = jnp.zeros((bp, D), F32).at[:B].set(c)
    tn = _tile(N, 512)
    out = pl.pallas_call(
        _adaln_kernel,
        grid=(L, N // tn),
        in_specs=[
            pl.BlockSpec((bp, D), lambda l, j: (0, 0)),
            pl.BlockSpec((None, D, tn), lambda l, j: (l, 0, j)),
            pl.BlockSpec((None, 1, tn), lambda l, j: (l, 0, j)),
        ],
        out_specs=pl.BlockSpec((None, bp, tn), lambda l, j: (l, 0, j)),
        out_shape=jax.ShapeDtypeStruct((L, bp, N), F32),
        compiler_params=_params(VMEM_LIMIT_MID, ("parallel", "parallel")),
        name="adaln",
    )(c_pad, ada_w, ada_b.reshape(L, 1, N))
    return out[:, :B]


def _rms(x):
    return x * lax.rsqrt(jnp.mean(x * x, axis=-1, keepdims=True) + EPS)


def _norm_mod_kernel(x_ref, g_ref, shift_ref, scale_ref, o_ref):
    y = _rms(x_ref[...]) * g_ref[...]
    o_ref[...] = (y * (1.0 + scale_ref[...]) + shift_ref[...]).astype(o_ref.dtype)


def _norm_mod(x, g, mod3):
    B, S, D = x.shape
    ts = _tile(S, NORM_ROWS)
    return pl.pallas_call(
        _norm_mod_kernel,
        grid=(B, S // ts),
        in_specs=[
            pl.BlockSpec((None, ts, D), lambda b, i: (b, i, 0)),
            pl.BlockSpec((1, D), lambda b, i: (0, 0)),
            pl.BlockSpec((None, 1, D), lambda b, i: (b, 0, 0)),
            pl.BlockSpec((None, 1, D), lambda b, i: (b, 0, 1)),
        ],
        out_specs=pl.BlockSpec((None, ts, D), lambda b, i: (b, i, 0)),
        out_shape=jax.ShapeDtypeStruct((B, S, D), BF16),
        compiler_params=_params(VMEM_LIMIT_MID, ("parallel", "parallel")),
        name="norm_mod",
    )(x, g.reshape(1, D), mod3, mod3)


def _norm_dual_kernel(x_ref, gkv_ref, g_ref, shift_ref, scale_ref, okv_ref, o_ref):
    y = _rms(x_ref[...])
    okv_ref[...] = (y * gkv_ref[...]).astype(okv_ref.dtype)
    o_ref[...] = ((y * g_ref[...]) * (1.0 + scale_ref[...]) + shift_ref[...]).astype(o_ref.dtype)


def _norm_dual(x, g_kv, g, mod3):
    B, S, D = x.shape
    ts = _tile(S, NORM_ROWS)
    row = pl.BlockSpec((None, ts, D), lambda b, i: (b, i, 0))
    vec = pl.BlockSpec((1, D), lambda b, i: (0, 0))
    return pl.pallas_call(
        _norm_dual_kernel,
        grid=(B, S // ts),
        in_specs=[row, vec, vec,
                  pl.BlockSpec((None, 1, D), lambda b, i: (b, 0, 0)),
                  pl.BlockSpec((None, 1, D), lambda b, i: (b, 0, 1))],
        out_specs=[row, row],
        out_shape=[jax.ShapeDtypeStruct((B, S, D), BF16)] * 2,
        compiler_params=_params(VMEM_LIMIT_BIG, ("parallel", "parallel")),
        name="norm_dual",
    )(x, g_kv.reshape(1, D), g.reshape(1, D), mod3, mod3)


def _final_norm_kernel(x_ref, g_ref, o_ref):
    o_ref[...] = _rms(x_ref[...]) * g_ref[...]


def _final_norm(x, g):
    B, S, D = x.shape
    ts = _tile(S, NORM_ROWS)
    row = pl.BlockSpec((None, ts, D), lambda b, i: (b, i, 0))
    return pl.pallas_call(
        _final_norm_kernel,
        grid=(B, S // ts),
        in_specs=[row, pl.BlockSpec((1, D), lambda b, i: (0, 0))],
        out_specs=row,
        out_shape=jax.ShapeDtypeStruct((B, S, D), F32),
        compiler_params=_params(VMEM_LIMIT_BIG, ("parallel", "parallel")),
        name="final_norm",
    )(x, g.reshape(1, D))


def _rope_table_kernel(inv_ref, cos_ref, sin_ref):
    ts = cos_ref.shape[0]
    pos = (pl.program_id(0) * ts + lax.broadcasted_iota(jnp.int32, (ts, HEAD_DIM), 0)).astype(F32)
    ang = pos * inv_ref[...]
    lane = lax.broadcasted_iota(jnp.int32, (ts, HEAD_DIM), 1)
    cos_ref[...] = jnp.cos(ang)
    sin_ref[...] = jnp.where(lane < HEAD_DIM // 2, -jnp.sin(ang), jnp.sin(ang))


def _rope_tables(S):
    half = HEAD_DIM // 2
    inv = ROPE_THETA ** (-jnp.arange(half, dtype=F32) / half)
    inv2 = jnp.concatenate([inv, inv]).reshape(1, HEAD_DIM)
    ts = _tile(S, 512)
    tab = pl.BlockSpec((ts, HEAD_DIM), lambda i: (i, 0))
    return pl.pallas_call(
        _rope_table_kernel,
        grid=(S // ts,),
        in_specs=[pl.BlockSpec((1, HEAD_DIM), lambda i: (0, 0))],
        out_specs=[tab, tab],
        out_shape=[jax.ShapeDtypeStruct((S, HEAD_DIM), F32)] * 2,
        name="rope_tables",
    )(inv2)


def _proj_kernel(*refs, n_lead, lead_scale, rope, w_rows):
    if rope:
        x_ref, w_ref, cos_ref, sin_ref, o_ref = refs
    else:
        x_ref, w_ref, o_ref = refs
    contract = (((1,), (1 if w_rows else 0,)), ((), ()))
    acc = lax.dot_general(x_ref[...], w_ref[...], contract, preferred_element_type=F32)
    tn = acc.shape[1]
    j = pl.program_id(1)

    def lead():
        cos, sin = cos_ref[...], sin_ref[...]
        for h in range(tn // HEAD_DIM):
            sl = slice(h * HEAD_DIM, (h + 1) * HEAD_DIM)
            xh = acc[:, sl]
            r = xh * cos + pltpu.roll(xh, HEAD_DIM // 2, axis=1) * sin
            o_ref[:, sl] = r.astype(o_ref.dtype)

    def rest():
        o_ref[...] = acc.astype(o_ref.dtype)

    if n_lead == 0:
        rest()
    elif rope:
        pl.when(j < n_lead)(lead)
        pl.when(j >= n_lead)(rest)
    else:
        scale = jnp.where(j < n_lead, lead_scale, 1.0).astype(F32)
        o_ref[...] = (acc * scale).astype(o_ref.dtype)


def _proj(x, w, *, w_rows=False, lead_cols=0, lead_scale=1.0, rope_tabs=None, seq=None,
          tn_pref=1024):
    M, K = x.shape
    N = w.shape[0] if w_rows else w.shape[1]
    tm = _tile(M if seq is None else seq, 1024)
    tn = _tile(N, tn_pref)
    assert lead_cols % tn == 0 and tn % HEAD_DIM == 0
    w_spec = (pl.BlockSpec((tn, K), lambda i, j: (j, 0)) if w_rows
              else pl.BlockSpec((K, tn), lambda i, j: (0, j)))
    in_specs = [pl.BlockSpec((tm, K), lambda i, j: (i, 0)), w_spec]
    args = [x, w]
    if rope_tabs is not None:
        spt = seq // tm
        tab = pl.BlockSpec((tm, HEAD_DIM), lambda i, j: (i % spt, 0))
        in_specs += [tab, tab]
        args += [t * lead_scale for t in rope_tabs] if lead_scale != 1.0 else list(rope_tabs)
    return pl.pallas_call(
        functools.partial(_proj_kernel, n_lead=lead_cols // tn, lead_scale=lead_scale,
                          rope=rope_tabs is not None, w_rows=w_rows),
        grid=(M // tm, N // tn),
        in_specs=in_specs,
        out_specs=pl.BlockSpec((tm, tn), lambda i, j: (i, j)),
        out_shape=jax.ShapeDtypeStruct((M, N), BF16),
        compiler_params=_params(VMEM_LIMIT_BIG, ("parallel", "parallel")),
        name="proj",
    )(*args)


def _cast_kernel(w_ref, o_ref):
    o_ref[...] = w_ref[...].astype(o_ref.dtype)


def _cast_rows(w_t, layer, n_rows):
    K = w_t.shape[2]
    tr = _tile(n_rows, 512)
    return pl.pallas_call(
        _cast_kernel,
        grid=(n_rows // tr,),
        in_specs=[pl.BlockSpec((None, tr, K), lambda i: (layer, i, 0))],
        out_specs=pl.BlockSpec((tr, K), lambda i: (i, 0)),
        out_shape=jax.ShapeDtypeStruct((n_rows, K), BF16),
        compiler_params=_params(VMEM_LIMIT_MID, ("parallel",)),
        name="cast_rows",
    )(w_t)


def _out_proj_kernel(a_ref, w_ref, x_ref, gate_ref, o_ref):
    y = jnp.dot(a_ref[...], w_ref[...], preferred_element_type=F32)
    o_ref[...] = x_ref[...] + gate_ref[...] * y


def _out_proj(a, w, x, mod3):
    B, S, N = x.shape
    K = a.shape[1]
    tm = _tile(S, 1024)
    tn = _tile(N, 512)
    spt = S // tm
    nj = N // tn
    out = pl.pallas_call(
        _out_proj_kernel,
        grid=(B * spt, nj),
        in_specs=[
            pl.BlockSpec((tm, K), lambda i, j: (i, 0)),
            pl.BlockSpec((K, tn), lambda i, j: (0, j)),
            pl.BlockSpec((tm, tn), lambda i, j: (i, j)),
            pl.BlockSpec((None, 1, tn), lambda i, j: (i // spt, 0, 2 * nj + j)),
        ],
        out_specs=pl.BlockSpec((tm, tn), lambda i, j: (i, j)),
        out_shape=jax.ShapeDtypeStruct((B * S, N), F32),
        compiler_params=_params(VMEM_LIMIT_BIG, ("parallel", "parallel")),
        name="out_proj",
    )(a, w, x.reshape(B * S, N), mod3)
    return out.reshape(B, S, N)


def _out_proj_norm_kernel(a_ref, w_ref, x_ref, gate_ref, g_ref, o_ref, *, nj):
    tn = w_ref.shape[1]
    j = pl.program_id(1)
    val = x_ref[...] + gate_ref[...] * jnp.dot(a_ref[...], w_ref[...], preferred_element_type=F32)
    for jj in range(nj):
        @pl.when(j == jj)
        def _(jj=jj):
            o_ref[:, jj * tn:(jj + 1) * tn] = val

    @pl.when(j == nj - 1)
    def _():
        o_ref[...] = _rms(o_ref[...]) * g_ref[...]


def _out_proj_norm(a, w, x, mod3, g):
    B, S, N = x.shape
    K = a.shape[1]
    tm = _tile(S, 512)
    tn = _tile(N, 512)
    spt = S // tm
    nj = N // tn
    out = pl.pallas_call(
        functools.partial(_out_proj_norm_kernel, nj=nj),
        grid=(B * spt, nj),
        in_specs=[
            pl.BlockSpec((tm, K), lambda i, j: (i, 0)),
            pl.BlockSpec((K, tn), lambda i, j: (0, j)),
            pl.BlockSpec((tm, tn), lambda i, j: (i, j)),
            pl.BlockSpec((None, 1, tn), lambda i, j: (i // spt, 0, 2 * nj + j)),
            pl.BlockSpec((1, N), lambda i, j: (0, 0)),
        ],
        out_specs=pl.BlockSpec((tm, N), lambda i, j: (i, 0)),
        out_shape=jax.ShapeDtypeStruct((B * S, N), F32),
        compiler_params=_params(VMEM_LIMIT_BIG, ("parallel", "arbitrary")),
        name="out_proj_norm",
    )(a, w, x.reshape(B * S, N), mod3, g.reshape(1, N))
    return out.reshape(B, S, N)


def _split3(x):
    hi = x.astype(BF16)
    r1 = x - hi.astype(F32)
    mid = r1.astype(BF16)
    lo = (r1 - mid.astype(F32)).astype(BF16)
    return hi, mid, lo


def _forget_kernel(h_ref, w_ref, b_ref, perm_ref, sel_ref, aq_ref, ak_ref, carry_ref, *, n_heads):
    ts = h_ref.shape[0]

    @pl.when(pl.program_id(1) == 0)
    def _():
        carry_ref[...] = jnp.zeros_like(carry_ref)

    w_t = w_ref[...].astype(BF16)
    w_t = jnp.concatenate([w_t, jnp.zeros((LANES - w_t.shape[0], w_t.shape[1]), BF16)], axis=0)
    f = lax.dot_general(h_ref[...], w_t, (((1,), (1,)), ((), ())),
                        preferred_element_type=F32) + b_ref[...]
    log_f = jnp.minimum(f, 0.0) - jnp.log1p(jnp.exp(-jnp.abs(f)))
    row = lax.broadcasted_iota(jnp.int32, (ts, ts), 0)
    col = lax.broadcasted_iota(jnp.int32, (ts, ts), 1)
    tri = jnp.where(row >= col, 1.0, 0.0).astype(BF16)
    cum = carry_ref[...]
    for part in _split3(log_f):
        cum = cum + jnp.dot(tri, part, preferred_element_type=F32)
    carry_ref[...] = cum[ts - 1:ts, :]

    parts = list(_split3(cum * LOG2E)) + [jnp.ones((ts, LANES), BF16)]
    packed = jnp.dot(jnp.concatenate(parts, axis=1), perm_ref[...],
                     preferred_element_type=F32).astype(BF16)
    for h in range(n_heads):
        r = jnp.dot(packed, sel_ref[h], preferred_element_type=F32)
        aq_ref[h] = r[:, :LANES].astype(BF16)
        ak_ref[h] = r[:, LANES:].astype(BF16)


def _side_column_selectors(n_heads):
    one = 3 * n_heads
    assert one < LANES
    perm = np.zeros((4 * LANES, LANES), np.float32)
    sel = np.zeros((n_heads, LANES, 2 * LANES), np.float32)
    perm[3 * LANES, one] = 1.0
    for h in range(n_heads):
        for p in range(3):
            perm[p * LANES + h, 3 * h + p] = 1.0
            sel[h, 3 * h + p, p] = 1.0
            sel[h, 3 * h + p, LANES + 3 + p] = -1.0
            sel[h, one, 3 + p] = 1.0
            sel[h, one, LANES + p] = 1.0
    return jnp.asarray(perm, BF16), jnp.asarray(sel, BF16)


def _forget_columns(h, w_t, layer, row0, b_f):
    B, S, D = h.shape
    H = b_f.shape[0]
    assert H <= LANES and H % BF16_SUBLANES == 0 and row0 % H == 0
    b_pad = jnp.zeros((1, LANES), F32).at[0, :H].set(b_f.astype(F32))
    ts = _tile(S, 256)
    out = pl.BlockSpec((None, H, ts, LANES), lambda b, i: (b, 0, i, 0))
    return pl.pallas_call(
        functools.partial(_forget_kernel, n_heads=H),
        grid=(B, S // ts),
        in_specs=[
            pl.BlockSpec((None, ts, D), lambda b, i: (b, i, 0)),
            pl.BlockSpec((None, H, D), lambda b, i: (layer, row0 // H, 0)),
            pl.BlockSpec((1, LANES), lambda b, i: (0, 0)),
            pl.BlockSpec((4 * LANES, LANES), lambda b, i: (0, 0)),
            pl.BlockSpec((H, LANES, 2 * LANES), lambda b, i: (0, 0, 0)),
        ],
        out_specs=[out, out],
        out_shape=[jax.ShapeDtypeStruct((B, H, S, LANES), BF16)] * 2,
        scratch_shapes=[pltpu.VMEM((1, LANES), F32)],
        compiler_params=_params(VMEM_LIMIT_MID, ("parallel", "arbitrary")),
        name="forget_columns",
    )(h, w_t, b_pad, *_side_column_selectors(H))


def _fox_kernel(q_ref, k_ref, v_ref, z_ref, aq_ref, ak_ref, o_ref, vt_ref, *, tq, hp):
    S = q_ref.shape[0]
    nq = S // tq
    hd = HEAD_DIM
    ones = jnp.ones((VT_ROWS - hd, tq), BF16)
    for g in range(hp):
        for j in range(nq):
            vblk = v_ref[j * tq:(j + 1) * tq, g * hd:(g + 1) * hd].astype(F32)
            vt_ref[g, j, 0:hd, :] = vblk.T.astype(BF16)
            vt_ref[g, j, hd:VT_ROWS, :] = ones

    key = lax.broadcasted_iota(jnp.int32, (tq, tq), 0)
    qry = lax.broadcasted_iota(jnp.int32, (tq, tq), 1)
    causal_t = key <= qry

    def scores(g, qa, ks, masked):
        ka = jnp.concatenate([k_ref[pl.ds(ks, tq), g * hd:(g + 1) * hd],
                              ak_ref[g, pl.ds(ks, tq), :]], axis=1)
        s_t = lax.dot_general(ka, qa, (((1,), (1,)), ((), ())), preferred_element_type=F32)
        return jnp.where(causal_t, s_t, -jnp.inf) if masked else s_t

    def update(g, j, s_t, carry):
        m, acc = carry
        m_new = jnp.maximum(m, jnp.max(s_t, axis=0, keepdims=True))
        alpha = jnp.exp2(m - m_new)
        p_t = jnp.exp2(s_t - m_new).astype(BF16)
        acc = alpha * acc + jnp.dot(vt_ref[g, j], p_t, preferred_element_type=F32)
        return m_new, acc

    def steps(qas, j, ks, carry, masked):
        s_ts = [scores(g, qas[g], ks, masked) for g in range(hp)]
        return tuple(update(g, j, s_ts[g], carry[g]) for g in range(hp))

    for i in range(nq):
        qs = i * tq
        qas = [jnp.concatenate([q_ref[qs:qs + tq, g * hd:(g + 1) * hd], aq_ref[g, qs:qs + tq, :]],
                               axis=1) for g in range(hp)]
        carry = tuple((jnp.full((1, tq), -jnp.inf, F32), jnp.zeros((VT_ROWS, tq), F32))
                      for _ in range(hp))
        s_cur = [scores(g, qas[g], 0, i == 0) for g in range(hp)]
        for j in range(i + 1):
            if j < i:
                s_nxt = [scores(g, qas[g], (j + 1) * tq, j + 1 == i) for g in range(hp)]
            carry = tuple(update(g, j, s_cur[g], carry[g]) for g in range(hp))
            if j < i:
                s_cur = s_nxt
        for g in range(hp):
            acc = carry[g][1]
            o = (acc[0:hd, :] * (1.0 / acc[hd:hd + 1, :])).T
            z = z_ref[qs:qs + tq, g * hd:(g + 1) * hd].astype(F32)
            o_ref[qs:qs + tq, g * hd:(g + 1) * hd] = (
                o * (z * jax.nn.sigmoid(z))).astype(o_ref.dtype)


def _fox_attention(qkvz, aq, ak, n_heads):
    B, S, _ = qkvz.shape
    H = n_heads
    hp = 2 if H % 2 == 0 else 1
    tq = _tile(S, 512)
    w = hp * HEAD_DIM
    col = lambda c: pl.BlockSpec((None, S, w), lambda b, h: (b, 0, c * (H // hp) + h))
    side = pl.BlockSpec((None, hp, S, LANES), lambda b, h: (b, h, 0, 0))
    return pl.pallas_call(
        functools.partial(_fox_kernel, tq=tq, hp=hp),
        grid=(B, H // hp),
        in_specs=[col(0), col(1), col(2), col(3), side, side],
        out_specs=pl.BlockSpec((None, S, w), lambda b, h: (b, 0, h)),
        out_shape=jax.ShapeDtypeStruct((B, S, H * HEAD_DIM), BF16),
        scratch_shapes=[pltpu.VMEM((hp, S // tq, VT_ROWS, tq), BF16)],
        compiler_params=_params(VMEM_LIMIT_BIG, ("parallel", "parallel")),
        name="fox_attention",
    )(qkvz, qkvz, qkvz, qkvz, aq, ak)


def _swa_kernel(q_ref, z_ref, k_ref, v_ref, sink_ref, o_ref, *, group):
    tq = q_ref.shape[0]
    blk = WINDOW
    hd = HEAD_DIM
    base = pl.program_id(2) * tq
    kk = lax.broadcasted_iota(jnp.int32, (2 * blk, blk), 0)
    qi = lax.broadcasted_iota(jnp.int32, (2 * blk, blk), 1)
    sink = sink_ref[...] * LOG2E
    ones = jnp.ones((VT_ROWS - hd, 2 * blk), BF16)
    for nb in range(tq // blk):
        start = base + nb * blk
        kstart = pl.multiple_of(jnp.maximum(start - blk, 0), blk)
        diff = qi + (start - kstart) - kk
        bias1 = jnp.where((diff >= 0) & (diff < WINDOW), 0.0, -jnp.inf)
        bias = jnp.concatenate([bias1] * group, axis=1)
        kb = k_ref[pl.ds(kstart, 2 * blk), :]
        vb = v_ref[pl.ds(kstart, 2 * blk), :].astype(F32)
        vt = jnp.concatenate([vb.T.astype(BF16), ones], axis=0)
        qt = q_ref[nb * blk:(nb + 1) * blk, :]
        qs = jnp.concatenate([qt[:, g * hd:(g + 1) * hd] for g in range(group)], axis=0)
        s_t = lax.dot_general(kb, qs, (((1,), (1,)), ((), ())), preferred_element_type=F32) + bias
        m = jnp.maximum(jnp.max(s_t, axis=0, keepdims=True), sink)
        p_t = jnp.exp2(s_t - m).astype(BF16)
        acc = jnp.dot(vt, p_t, preferred_element_type=F32)
        denom = acc[hd:hd + 1, :] + jnp.exp2(sink - m)
        o_t = acc[0:hd, :] * (1.0 / denom)
        for g in range(group):
            sl = slice(g * hd, (g + 1) * hd)
            z = z_ref[nb * blk:(nb + 1) * blk, sl].astype(F32)
            o_ref[nb * blk:(nb + 1) * blk, sl] = (
                o_t[:, sl].T * (z * jax.nn.sigmoid(z))).astype(o_ref.dtype)


def _swa_attention(qz, kv, sinks):
    B, S, W2 = qz.shape
    width = W2 // 2
    hq = width // HEAD_DIM
    group = hq // KV_HEADS
    gw = group * HEAD_DIM
    tq = _tile(S, 512)
    sink_b = jnp.broadcast_to(sinks.astype(F32).reshape(KV_HEADS, 1, group, 1),
                              (KV_HEADS, 1, group, LANES)).reshape(KV_HEADS, 1, group * LANES)
    return pl.pallas_call(
        functools.partial(_swa_kernel, group=group),
        grid=(B, KV_HEADS, S // tq),
        in_specs=[
            pl.BlockSpec((None, tq, gw), lambda b, g, i: (b, i, g)),
            pl.BlockSpec((None, tq, gw), lambda b, g, i: (b, i, KV_HEADS + g)),
            pl.BlockSpec((None, S, HEAD_DIM), lambda b, g, i: (b, 0, g)),
            pl.BlockSpec((None, S, HEAD_DIM), lambda b, g, i: (b, 0, KV_HEADS + g)),
            pl.BlockSpec((None, 1, group * LANES), lambda b, g, i: (g, 0, 0)),
        ],
        out_specs=pl.BlockSpec((None, tq, gw), lambda b, g, i: (b, i, g)),
        out_shape=jax.ShapeDtypeStruct((B, S, width), BF16),
        compiler_params=_params(VMEM_LIMIT_MID, ("parallel", "parallel", "parallel")),
        name="swa_attention",
    )(qz, qz, kv, kv, sink_b)


def kernel(x, c, norm_g, ada_w, ada_b, a_w_in, a_b_f, a_w_out, kv_norm_g, kv_w,
           b_w_in, b_sinks, b_w_out, final_norm_g):
    B, S, D = x.shape
    n_a = a_w_in.shape[0]
    n_b = b_w_in.shape[0]
    depth = n_a + n_b
    sm_scale = HEAD_DIM ** -0.5 * LOG2E

    mod = _adaln(c, ada_w, ada_b)
    rope_tabs = _rope_tables(S)

    h = x
    kv = None
    for layer in range(depth):
        mod3 = mod[layer].reshape(B, 1, 3 * D)
        if layer < n_a:
            a_heads = a_b_f.shape[1]
            a_width = a_heads * HEAD_DIM
            hn = _norm_mod(h, norm_g[layer], mod3)
            w_in_t = jnp.swapaxes(a_w_in, 1, 2)
            qkvz = _proj(hn.reshape(B * S, D), _cast_rows(w_in_t, layer, 4 * a_width),
                         w_rows=True, lead_cols=a_width, lead_scale=sm_scale)
            aq, ak = _forget_columns(hn, w_in_t, layer, 4 * a_width, a_b_f[layer])
            o = _fox_attention(qkvz.reshape(B, S, 4 * a_width), aq, ak, a_heads)
            h = _out_proj(o.reshape(B * S, a_width), a_w_out[layer].astype(BF16), h, mod3)
        else:
            j = layer - n_a
            if kv is None:
                hkv, hn = _norm_dual(h, kv_norm_g, norm_g[layer], mod3)
                kvw = kv_w.shape[1] // 2
                kv = _proj(hkv.reshape(B * S, D), kv_w.astype(BF16), lead_cols=kvw,
                           rope_tabs=rope_tabs, seq=S, tn_pref=kvw).reshape(B, S, 2 * kvw)
            else:
                hn = _norm_mod(h, norm_g[layer], mod3)
            b_width = b_w_in.shape[2] // 2
            qz = _proj(hn.reshape(B * S, D), b_w_in[j].astype(BF16), lead_cols=b_width,
                       lead_scale=sm_scale, rope_tabs=rope_tabs, seq=S)
            o = _swa_attention(qz.reshape(B, S, 2 * b_width), kv, b_sinks[j])
            if layer == depth - 1:
                return _out_proj_norm(o.reshape(B * S, b_width), b_w_out[j].astype(BF16), h,
                                      mod3, final_norm_g)
            h = _out_proj(o.reshape(B * S, b_width), b_w_out[j].astype(BF16), h, mod3)
    return _final_norm(h, final_norm_g)
```
